```python
import jax, jax.numpy as jnp
from jax import lax
import numpy as np

D_MODEL = 1024
BATCH = 1
SEQ = 16384
DEPTH = 1

ATTN_HEADS = 8
ATTN_HEAD_DIM = 64
ATTN_WIDTH = ATTN_HEADS * ATTN_HEAD_DIM
MOBA_BLOCK = 256
MOBA_TOPK = 3
Q_BLOCK = 128
MLSTM_HEADS = 4
MLSTM_HEAD_DIM = 128
MLSTM_WIDTH = MLSTM_HEADS * MLSTM_HEAD_DIM
MLSTM_CHUNK = 128
CONV_K = 4
SPLIT_SIZES = (ATTN_WIDTH, ATTN_WIDTH, ATTN_WIDTH,
               MLSTM_WIDTH, MLSTM_WIDTH, MLSTM_WIDTH, MLSTM_WIDTH,
               MLSTM_HEADS, MLSTM_HEADS,
               D_MODEL, D_MODEL)
IN_WIDTH = sum(SPLIT_SIZES)
N_GROUPS = 4
EXPERTS_PER_GROUP = 8
N_EXPERTS = N_GROUPS * EXPERTS_PER_GROUP
TOPK_IN_GROUP = 2
EXPERT_FF = 512
ROW_BLOCK = 256
PLE_DIM = 256
EPS = 1e-6

kernel_name = "hybrid_moba_mlstm_hmoe_block"


def rmsnorm(x, g):
    xf = x.astype(jnp.float32)
    y = xf * lax.rsqrt(jnp.mean(xf * xf, axis=-1, keepdims=True) + EPS)
    return (y * g.astype(jnp.float32)).astype(x.dtype)


def split_cols(t, sizes):
    outs, start = [], 0
    for s in sizes:
        outs.append(t[..., start:start + s])
        start += s
    return outs


def to_heads(t, n_heads):
    b, s, _ = t.shape
    return t.reshape(b, s, n_heads, -1).transpose(0, 2, 1, 3)


def from_heads(t):
    b, h, s, d = t.shape
    return t.transpose(0, 2, 1, 3).reshape(b, s, h * d)


def causal_dwconv(x, w, b):
    s = x.shape[1]
    xp = jnp.pad(x, ((0, 0), (CONV_K - 1, 0), (0, 0)))
    y = b
    for k in range(CONV_K):
        y = y + xp[:, k:k + s, :] * w[k]
    return y


def moba_attention(q, k, v):
    b, h, s, d = q.shape
    nb = -(-s // MOBA_BLOCK)
    pad = nb * MOBA_BLOCK - s
    kp = jnp.pad(k, ((0, 0), (0, 0), (0, pad), (0, 0)))
    vp = jnp.pad(v, ((0, 0), (0, 0), (0, pad), (0, 0)))
    kb = kp.reshape(b, h, nb, MOBA_BLOCK, d)
    vb = vp.reshape(b, h, nb, MOBA_BLOCK, d)
    kmean = jnp.mean(kb.astype(jnp.float32), axis=3)
    topk = min(MOBA_TOPK, nb)
    scale = ATTN_HEAD_DIM ** -0.5
    b_idx = jnp.arange(b)[:, None, None, None]
    h_idx = jnp.arange(h)[None, :, None, None]
    n_qblocks = s // Q_BLOCK

    def block_fn(qi):
        q0 = qi * Q_BLOCK
        qb = lax.dynamic_slice_in_dim(q, q0, Q_BLOCK, axis=2)
        qpos = q0 + jnp.arange(Q_BLOCK)
        cur = q0 // MOBA_BLOCK
        gate = jnp.einsum('bhqd,bhnd->bhqn', qb.astype(jnp.float32), kmean)
        gate = jnp.where(jnp.arange(nb) < cur, gate, -jnp.inf)
        _, sel = lax.top_k(gate, topk)
        sel_valid = jnp.arange(topk) < cur
        ks = kb[b_idx, h_idx, sel]
        vs = vb[b_idx, h_idx, sel]
        s_sel = jnp.einsum('bhqd,bhqjkd->bhqjk', qb, ks).astype(jnp.float32) * scale
        s_sel = jnp.where(sel_valid[:, None], s_sel, -jnp.inf)
        s_sel = s_sel.reshape(b, h, Q_BLOCK, topk * MOBA_BLOCK)
        ko = lax.dynamic_slice_in_dim(kp, cur * MOBA_BLOCK, MOBA_BLOCK, axis=2)
        vo = lax.dynamic_slice_in_dim(vp, cur * MOBA_BLOCK, MOBA_BLOCK, axis=2)
        kpos = cur * MOBA_BLOCK + jnp.arange(MOBA_BLOCK)
        s_own = jnp.einsum('bhqd,bhkd->bhqk', qb, ko).astype(jnp.float32) * scale
        s_own = jnp.where(kpos[None, :] <= qpos[:, None], s_own, -jnp.inf)
        probs = jax.nn.softmax(jnp.concatenate([s_sel, s_own], axis=-1), axis=-1)
        p_sel = probs[..., :topk * MOBA_BLOCK].reshape(b, h, Q_BLOCK, topk, MOBA_BLOCK).astype(v.dtype)
        p_own = probs[..., topk * MOBA_BLOCK:].astype(v.dtype)
        return (jnp.einsum('bhqjk,bhqjkd->bhqd', p_sel, vs)
                + jnp.einsum('bhqk,bhkd->bhqd', p_own, vo))

    out = lax.map(block_fn, jnp.arange(n_qblocks))
    return jnp.moveaxis(out, 0, 2).reshape(b, h, s, d)


def mlstm_chunkwise(q, k, v, i_pre, f_pre):
    b, h, s, dk = q.shape
    dv = v.shape[-1]
    L = MLSTM_CHUNK
    nc = s // L
    f32 = jnp.float32
    log_f = jax.nn.log_sigmoid(f_pre.astype(f32))
    log_i = i_pre.astype(f32)

    def chunks(t):
        return jnp.moveaxis(t.reshape(b, h, nc, L, *t.shape[3:]), 2, 0)

    xs = (chunks(q.astype(f32)), chunks(k.astype(f32)), chunks(v.astype(f32)), chunks(log_i), chunks(log_f))
    causal = jnp.tril(jnp.ones((L, L), dtype=bool))

    def step(carry, inp):
        C, n, m = carry
        qc, kc, vc, lic, lfc = inp
        bcum = jnp.cumsum(lfc, axis=-1)
        dmat = bcum[..., :, None] - bcum[..., None, :] + lic[..., None, :]
        dmat = jnp.where(causal, dmat, -jnp.inf)
        inter = bcum + m[..., None]
        m_t = jnp.maximum(inter, jnp.max(dmat, axis=-1))
        w_ts = jnp.exp(dmat - m_t[..., None])
        sc_inter = jnp.exp(inter - m_t)
        sqk = jnp.einsum('bhtd,bhsd->bhts', qc, kc) * w_ts
        num = (jnp.einsum('bhts,bhsv->bhtv', sqk, vc)
               + sc_inter[..., None] * jnp.einsum('bhtd,bhdv->bhtv', qc, C))
        den = jnp.sum(sqk, axis=-1) + sc_inter * jnp.einsum('bhtd,bhd->bht', qc, n)
        h_t = num / jnp.maximum(jnp.abs(den), jnp.exp(-m_t))[..., None]
        b_last = bcum[..., -1]
        lw = b_last[..., None] - bcum + lic
        m_new = jnp.maximum(b_last + m, jnp.max(lw, axis=-1))
        w_s = jnp.exp(lw - m_new[..., None])
        decay = jnp.exp(b_last + m - m_new)
        C_new = decay[..., None, None] * C + jnp.einsum('bhsd,bhsv->bhdv', kc * w_s[..., None], vc)
        n_new = decay[..., None] * n + jnp.einsum('bhs,bhsd->bhd', w_s, kc)
        return (C_new, n_new, m_new), h_t

    init = (jnp.zeros((b, h, dk, dv), f32), jnp.zeros((b, h, dk), f32), jnp.zeros((b, h), f32))
    _, hs = lax.scan(step, init, xs)
    return jnp.moveaxis(hs, 0, 2).reshape(b, h, s, dv)


def hierarchical_moe(xn, rg_w, rg_b, re_w, re_b, w_gate, w_up, w_down):
    b, s, d = xn.shape
    T = b * s
    xt = xn.reshape(T, d)
    g_logits = (xt @ rg_w + rg_b).astype(jnp.float32)
    g_prob = jax.nn.softmax(g_logits, axis=-1)
    grp = jnp.argmax(g_logits, axis=-1)
    p_grp = jnp.take_along_axis(g_prob, grp[:, None], axis=-1)[:, 0]
    e_logits = (xt @ re_w + re_b).astype(jnp.float32).reshape(T, N_GROUPS, EXPERTS_PER_GROUP)
    e_logits = jnp.take_along_axis(e_logits, grp[:, None, None], axis=1)[:, 0]
    e_prob = jax.nn.softmax(e_logits, axis=-1)
    w_top, i_top = lax.top_k(e_prob, TOPK_IN_GROUP)
    w_top = w_top / jnp.sum(w_top, axis=-1, keepdims=True) * p_grp[:, None]
    eid = grp[:, None] * EXPERTS_PER_GROUP + i_top

    A = T * TOPK_IN_GROUP
    flat_e = eid.reshape(A)
    flat_tok = jnp.repeat(jnp.arange(T, dtype=jnp.int32), TOPK_IN_GROUP)
    flat_w = w_top.reshape(A)
    order = jnp.argsort(flat_e)
    se, stok, sw = flat_e[order], flat_tok[order], flat_w[order]
    counts = jax.ops.segment_sum(jnp.ones((A,), jnp.int32), flat_e, num_segments=N_EXPERTS)
    offsets = jnp.cumsum(counts) - counts
    pcounts = (counts + ROW_BLOCK - 1) // ROW_BLOCK * ROW_BLOCK
    pends = jnp.cumsum(pcounts)
    poffs = pends - pcounts
    dest = poffs[se] + (jnp.arange(A, dtype=jnp.int32) - offsets[se])
    n_pad = (A + N_EXPERTS * (ROW_BLOCK - 1) + ROW_BLOCK - 1) // ROW_BLOCK * ROW_BLOCK
    n_blk = n_pad // ROW_BLOCK
    row_tok = jnp.zeros((n_pad,), jnp.int32).at[dest].set(stok)
    row_w = jnp.zeros((n_pad,), jnp.float32).at[dest].set(sw)
    blk_e = jnp.minimum(jnp.searchsorted(pends, jnp.arange(n_blk) * ROW_BLOCK, side='right'),
                        N_EXPERTS - 1)

    def blk(args):
        bi, e = args
        toks = lax.dynamic_slice_in_dim(row_tok, bi * ROW_BLOCK, ROW_BLOCK)
        wr = lax.dynamic_slice_in_dim(row_w, bi * ROW_BLOCK, ROW_BLOCK)
        xb = xt[toks]
        hid = jax.nn.silu(xb @ w_gate[e]) * (xb @ w_up[e])
        return ((hid @ w_down[e]) * wr[:, None]).astype(xt.dtype)

    yb = lax.map(blk, (jnp.arange(n_blk), blk_e))
    y = jax.ops.segment_sum(yb.reshape(n_pad, d), row_tok, num_segments=T)
    return y.reshape(b, s, d)


def setup_inputs(seed: int = 0) -> dict:
    key = jax.random.key(seed)
    ks = jax.random.split(key, 24)
    nrm = jax.random.normal
    L = DEPTH
    f32 = jnp.float32

    def gain(k, n):
        return 1.0 + 0.01 * nrm(k, (L, n), f32)

    if_b = jnp.concatenate([0.1 * nrm(ks[6], (L, MLSTM_HEADS), f32),
                            3.0 + 3.0 * jax.random.uniform(ks[7], (L, MLSTM_HEADS), f32)], axis=-1)
    return {
        "x": nrm(ks[0], (BATCH, SEQ, D_MODEL), f32),
        "p": nrm(ks[1], (DEPTH, BATCH, SEQ, PLE_DIM), f32),
        "mix_norm_g": gain(ks[2], D_MODEL),
        "w_in": nrm(ks[3], (L, D_MODEL, IN_WIDTH), f32) * D_MODEL ** -0.5,
        "b_gate": 0.01 * nrm(ks[4], (L, 2 * D_MODEL), f32),
        "conv_w": nrm(ks[5], (L, CONV_K, 2 * MLSTM_WIDTH), f32) * CONV_K ** -0.5,
        "conv_b": 0.01 * nrm(ks[8], (L, 2 * MLSTM_WIDTH), f32),
        "mlstm_if_b": if_b,
        "mlstm_norm_g": gain(ks[9], MLSTM_WIDTH),
        "w_up_attn": nrm(ks[10], (L, ATTN_WIDTH, D_MODEL), f32) * ATTN_WIDTH ** -0.5,
        "w_up_mlstm": nrm(ks[11], (L, MLSTM_WIDTH, D_MODEL), f32) * MLSTM_WIDTH ** -0.5,
        "w_out": nrm(ks[12], (L, D_MODEL, D_MODEL), f32) * D_MODEL ** -0.5,
        "ffn_norm_g": gain(ks[13], D_MODEL),
        "router_group_w": nrm(ks[14], (L, D_MODEL, N_GROUPS), f32) * D_MODEL ** -0.5,
        "router_group_b": 0.01 * nrm(ks[15], (L, N_GROUPS), f32),
        "router_expert_w": nrm(ks[16], (L, D_MODEL, N_EXPERTS), f32) * D_MODEL ** -0.5,
        "router_expert_b": 0.01 * nrm(ks[17], (L, N_EXPERTS), f32),
        "expert_w_gate": nrm(ks[18], (L, N_EXPERTS, D_MODEL, EXPERT_FF), f32) * D_MODEL ** -0.5,
        "expert_w_up": nrm(ks[19], (L, N_EXPERTS, D_MODEL, EXPERT_FF), f32) * D_MODEL ** -0.5,
        "expert_w_down": nrm(ks[20], (L, N_EXPERTS, EXPERT_FF, D_MODEL), f32) * EXPERT_FF ** -0.5,
        "ple_norm_g": gain(ks[21], D_MODEL),
        "w_ple_gate": nrm(ks[22], (L, D_MODEL, D_MODEL), f32) * D_MODEL ** -0.5,
        "w_ple_proj": nrm(ks[23], (L, PLE_DIM, D_MODEL), f32) * PLE_DIM ** -0.5,
        "final_norm_g": 1.0 + 0.01 * nrm(jax.random.fold_in(key, 99), (D_MODEL,), f32),
    }


def reference(x, p, mix_norm_g, w_in, b_gate, conv_w, conv_b, mlstm_if_b, mlstm_norm_g,
              w_up_attn, w_up_mlstm, w_out, ffn_norm_g, router_group_w, router_group_b,
              router_expert_w, router_expert_b, expert_w_gate, expert_w_up, expert_w_down,
              ple_norm_g, w_ple_gate, w_ple_proj, final_norm_g):
    h = x
    for i in range(DEPTH):
        xn = rmsnorm(h, mix_norm_g[i])
        proj = xn @ w_in[i]
        aq, ak, av, mq, mk, mv, mo, mi, mf, ga, gm = split_cols(proj, SPLIT_SIZES)
        ya = from_heads(moba_attention(to_heads(aq, ATTN_HEADS), to_heads(ak, ATTN_HEADS),
                                       to_heads(av, ATTN_HEADS)))
        qk = jax.nn.silu(causal_dwconv(jnp.concatenate([mq, mk], axis=-1), conv_w[i], conv_b[i]))
        mq_c, mk_c = qk[..., :MLSTM_WIDTH], qk[..., MLSTM_WIDTH:]
        i_pre = (mi + mlstm_if_b[i, :MLSTM_HEADS]).transpose(0, 2, 1)
        f_pre = (mf + mlstm_if_b[i, MLSTM_HEADS:]).transpose(0, 2, 1)
        h_tilde = mlstm_chunkwise(to_heads(mq_c, MLSTM_HEADS),
                                  to_heads(mk_c, MLSTM_HEADS) * MLSTM_HEAD_DIM ** -0.5,
                                  to_heads(mv, MLSTM_HEADS), i_pre, f_pre)
        h_cell = jax.nn.sigmoid(to_heads(mo, MLSTM_HEADS).astype(jnp.float32)) * h_tilde
        h_cell = rmsnorm(h_cell, mlstm_norm_g[i].reshape(MLSTM_HEADS, 1, MLSTM_HEAD_DIM))
        ym = from_heads(h_cell).astype(h.dtype)
        gates = jax.nn.sigmoid(jnp.concatenate([ga, gm], axis=-1) + b_gate[i])
        merged = (gates[..., :D_MODEL] * (ya @ w_up_attn[i])
                  + gates[..., D_MODEL:] * (ym @ w_up_mlstm[i]))
        h = h + merged @ w_out[i]
        h = h + hierarchical_moe(rmsnorm(h, ffn_norm_g[i]), router_group_w[i], router_group_b[i],
                                 router_expert_w[i], router_expert_b[i], expert_w_gate[i],
                                 expert_w_up[i], expert_w_down[i])
        ple_gate = jax.nn.sigmoid(rmsnorm(h, ple_norm_g[i]) @ w_ple_gate[i])
        h = h + (p[i] @ w_ple_proj[i]) * ple_gate
    return rmsnorm(h, final_norm_g)
```

```python
import functools

import jax
import jax.numpy as jnp
from jax import lax
from jax.experimental import pallas as pl
from jax.experimental.pallas import tpu as pltpu

F32 = jnp.float32
BF16 = jnp.bfloat16
I32 = jnp.int32

D_MODEL = 1024
ATTN_HEADS = 8
ATTN_HEAD_DIM = 64
ATTN_WIDTH = ATTN_HEADS * ATTN_HEAD_DIM
MOBA_BLOCK = 256
MOBA_TOPK = 3
MLSTM_HEADS = 4
MLSTM_HEAD_DIM = 128
MLSTM_WIDTH = MLSTM_HEADS * MLSTM_HEAD_DIM
MLSTM_CHUNK = 128
CONV_K = 4
N_GROUPS = 4
EXPERTS_PER_GROUP = 8
N_EXPERTS = N_GROUPS * EXPERTS_PER_GROUP
EXPERT_FF = 512
ROW_BLOCK = 256
PLE_DIM = 256
EPS = 1e-6

LANES = 128
SUBLANES = 8
VMEM_LIMIT = 56 * 1024 * 1024

TM_PROJ = 512
TM_MERGE = 512
TM_TOK = 256
IDX_TOK = 1024

NT_DIMS = (((1,), (1,)), ((), ()))
TN_DIMS = (((0,), (0,)), ((), ()))
NEG_INF = float("-inf")


def _cparams(*sem):
    return pltpu.CompilerParams(dimension_semantics=sem, vmem_limit_bytes=VMEM_LIMIT)


def _rms(x, g):
    return x * lax.rsqrt(jnp.mean(x * x, axis=-1, keepdims=True) + EPS) * g


def _sigmoid(x):
    return 1.0 / (1.0 + jnp.exp(-x))


_C_K, _C_QK, _C_V, _C_O, _C_G, _C_IF, _C_END = 0, 512, 1536, 2048, 2560, 4608, 4736


def _proj_kernel(x_ref, g_ref, wrow_ref, wt_ref,
                 k_ref, kmean_ref, mqk_ref, mv_ref, mo_ref, gates_ref, ifc_ref,
                 qT_ref, vT_ref, ifT_ref):
    xb = _rms(x_ref[...], g_ref[...]).astype(BF16)

    def rowdot(a, b):
        return jnp.dot(xb, wrow_ref[:, a:b], preferred_element_type=F32)

    k = rowdot(_C_K, _C_QK)
    k_ref[...] = k.astype(BF16)
    for b in range(TM_PROJ // MOBA_BLOCK):
        kmean_ref[b] = jnp.mean(k[b * MOBA_BLOCK:(b + 1) * MOBA_BLOCK], axis=0, keepdims=True)
    mqk_ref[...] = rowdot(_C_QK, _C_V).astype(BF16)
    mv_ref[...] = rowdot(_C_V, _C_O).astype(BF16)
    mo_ref[...] = rowdot(_C_O, _C_G).astype(BF16)
    gates_ref[...] = rowdot(_C_G, _C_IF).astype(BF16)
    ifc_ref[...] = rowdot(_C_IF, _C_END)

    def colT(a, b):
        return lax.dot_general(wt_ref[a:b, :], xb, NT_DIMS, preferred_element_type=F32)

    qT = colT(0, ATTN_WIDTH) * (ATTN_HEAD_DIM ** -0.5)
    vT = colT(ATTN_WIDTH, 2 * ATTN_WIDTH)
    for b in range(TM_PROJ // MOBA_BLOCK):
        qT_ref[b] = qT[:, b * MOBA_BLOCK:(b + 1) * MOBA_BLOCK].astype(BF16)
        vT_ref[b] = vT[:, b * MOBA_BLOCK:(b + 1) * MOBA_BLOCK].astype(BF16)
    ifT_ref[...] = colT(2 * ATTN_WIDTH, 2 * ATTN_WIDTH + 16)[0:SUBLANES]


def _proj(x, g, wrow, wt):
    s = x.shape[0]
    nb = s // MOBA_BLOCK
    bpt = TM_PROJ // MOBA_BLOCK
    row = lambda w: pl.BlockSpec((TM_PROJ, w), lambda i: (i, 0))
    full = lambda a: pl.BlockSpec(a.shape, lambda i: (0,) * a.ndim)
    out_shape = (
        jax.ShapeDtypeStruct((s, ATTN_WIDTH), BF16),
        jax.ShapeDtypeStruct((nb, 1, ATTN_WIDTH), F32),
        jax.ShapeDtypeStruct((s, 2 * MLSTM_WIDTH), BF16),
        jax.ShapeDtypeStruct((s, MLSTM_WIDTH), BF16),
        jax.ShapeDtypeStruct((s, MLSTM_WIDTH), BF16),
        jax.ShapeDtypeStruct((s, 2 * D_MODEL), BF16),
        jax.ShapeDtypeStruct((s, LANES), F32),
        jax.ShapeDtypeStruct((nb, ATTN_WIDTH, MOBA_BLOCK), BF16),
        jax.ShapeDtypeStruct((nb, ATTN_WIDTH, MOBA_BLOCK), BF16),
        jax.ShapeDtypeStruct((SUBLANES, s), F32),
    )
    out_specs = (
        row(ATTN_WIDTH),
        pl.BlockSpec((bpt, 1, ATTN_WIDTH), lambda i: (i, 0, 0)),
        row(2 * MLSTM_WIDTH), row(MLSTM_WIDTH), row(MLSTM_WIDTH), row(2 * D_MODEL), row(LANES),
        pl.BlockSpec((bpt, ATTN_WIDTH, MOBA_BLOCK), lambda i: (i, 0, 0)),
        pl.BlockSpec((bpt, ATTN_WIDTH, MOBA_BLOCK), lambda i: (i, 0, 0)),
        pl.BlockSpec((SUBLANES, TM_PROJ), lambda i: (0, i)),
    )
    return pl.pallas_call(
        _proj_kernel, grid=(s // TM_PROJ,),
        in_specs=[row(D_MODEL), full(g), full(wrow), full(wt)],
        out_specs=out_specs, out_shape=out_shape,
        compiler_params=_cparams("arbitrary"), name="proj",
    )(x, g, wrow, wt)


def _moba_kernel(qT_ref, k_ref, vT_ref, km_ref, o_ref, bias_ref):
    i = pl.program_id(1)
    nb = k_ref.shape[0]
    hd = ATTN_HEAD_DIM
    qT = qT_ref[0]
    row = lax.broadcasted_iota(I32, qT.shape, 0)
    km = km_ref[...]
    km_hi = km.astype(BF16)
    km_lo = (km - km_hi.astype(F32)).astype(BF16)
    blk = lax.broadcasted_iota(I32, (nb, MOBA_BLOCK), 0)

    qms = []
    for hh in range(2):
        qm = jnp.where((row >= hh * hd) & (row < (hh + 1) * hd), qT, jnp.zeros_like(qT))
        qms.append(qm)
        gate = (jnp.dot(km_hi, qm, preferred_element_type=F32)
                + jnp.dot(km_lo, qm, preferred_element_type=F32))
        g = jnp.where(blk < i, gate, NEG_INF)
        sel = jnp.zeros(g.shape, F32)
        for _ in range(MOBA_TOPK):
            mx = jnp.max(g, axis=0, keepdims=True)
            first = jnp.min(jnp.where(g == mx, blk, nb), axis=0, keepdims=True)
            pick = (blk == first) & (mx > NEG_INF)
            sel = jnp.where(pick, 1.0, sel)
            g = jnp.where(pick, NEG_INF, g)
        bias_ref[hh] = jnp.where(sel > 0.0, 0.0, NEG_INF)
    q2 = jnp.concatenate(qms, axis=1)

    kpos = lax.broadcasted_iota(I32, (MOBA_BLOCK, MOBA_BLOCK), 0)
    qpos = lax.broadcasted_iota(I32, (MOBA_BLOCK, MOBA_BLOCK), 1)
    causal = kpos <= qpos

    def scores(j):
        return jnp.dot(k_ref[j], q2, preferred_element_type=F32)

    def pv(j, hh, p):
        return jnp.dot(vT_ref[j, hh * hd:(hh + 1) * hd, :], p.astype(BF16),
                       preferred_element_type=F32)

    s_own = scores(i)
    carry = []
    for hh in range(2):
        st = jnp.where(causal, s_own[:, hh * MOBA_BLOCK:(hh + 1) * MOBA_BLOCK], NEG_INF)
        m = jnp.max(st, axis=0, keepdims=True)
        p = jnp.exp(st - m)
        carry += [m, jnp.sum(p, axis=0, keepdims=True), pv(i, hh, p)]

    def body(j, carry):
        s_j = scores(j)
        out = []
        for hh in range(2):
            m, l, acc = carry[3 * hh:3 * hh + 3]
            st = s_j[:, hh * MOBA_BLOCK:(hh + 1) * MOBA_BLOCK] + bias_ref[hh, pl.ds(j, 1), :]
            m_new = jnp.maximum(m, jnp.max(st, axis=0, keepdims=True))
            p = jnp.exp(st - m_new)
            alpha = jnp.exp(m - m_new)
            out += [m_new, alpha * l + jnp.sum(p, axis=0, keepdims=True), alpha * acc + pv(j, hh, p)]
        return tuple(out)

    carry = lax.fori_loop(0, i, body, tuple(carry))
    oT = jnp.concatenate([carry[2] / carry[1], carry[5] / carry[4]], axis=0)
    o_ref[...] = oT.T.astype(BF16)


def _moba(qT, k3, vT, km):
    nb = k3.shape[0]
    s = nb * MOBA_BLOCK
    pw = 2 * ATTN_HEAD_DIM
    return pl.pallas_call(
        _moba_kernel, grid=(ATTN_HEADS // 2, nb),
        in_specs=[
            pl.BlockSpec((1, pw, MOBA_BLOCK), lambda p, i: (i, p, 0)),
            pl.BlockSpec((nb, MOBA_BLOCK, pw), lambda p, i: (0, 0, p)),
            pl.BlockSpec((nb, pw, MOBA_BLOCK), lambda p, i: (0, p, 0)),
            pl.BlockSpec((nb, pw), lambda p, i: (0, p)),
        ],
        out_specs=pl.BlockSpec((MOBA_BLOCK, pw), lambda p, i: (i, p)),
        out_shape=jax.ShapeDtypeStruct((s, ATTN_WIDTH), BF16),
        scratch_shapes=[pltpu.VMEM((2, nb, MOBA_BLOCK), F32)],
        compiler_params=_cparams("arbitrary", "arbitrary"), name="moba",
    )(qT, k3, vT, km)


def _log_sigmoid(x):
    return -(jnp.maximum(-x, 0.0) + jnp.log1p(jnp.exp(-jnp.abs(x))))


def _mlstm_kernel(qk_ref, v_ref, o_ref, ifc_ref, ifT_ref, cw_ref, cb_ref, ifbr_ref, ifbc_ref, ng_ref,
                  y_ref, ext_ref, c_ref, m_ref):
    L, H, dh = MLSTM_CHUNK, MLSTM_HEADS, MLSTM_HEAD_DIM
    halo = SUBLANES

    @pl.when(pl.program_id(0) == 0)
    def _():
        ext_ref[0:halo, :] = jnp.zeros((halo, 2 * MLSTM_WIDTH), F32)
        c_ref[...] = jnp.zeros(c_ref.shape, F32)
        m_ref[...] = jnp.zeros(m_ref.shape, F32)

    ext_ref[halo:halo + L, :] = qk_ref[...].astype(F32)
    y = cb_ref[...]
    for kk in range(CONV_K):
        off = halo - (CONV_K - 1) + kk
        y = y + ext_ref[off:off + L, :] * cw_ref[kk:kk + 1, :]
    ext_ref[0:halo, :] = ext_ref[L:L + halo, :]
    qk = y * _sigmoid(y)

    r_i = lax.broadcasted_iota(I32, (L, L), 0)
    c_i = lax.broadcasted_iota(I32, (L, L), 1)
    tril = (c_i <= r_i)
    tril_f = tril.astype(F32)
    triu_f = (r_i <= c_i).astype(F32)
    hp = lax.Precision.HIGHEST

    gc = ifc_ref[...] + ifbr_ref[...]
    gr = ifT_ref[...] + ifbc_ref[...]
    bcum_c = jnp.dot(tril_f, _log_sigmoid(gc), precision=hp, preferred_element_type=F32)
    bcum_r = jnp.dot(_log_sigmoid(gr), triu_f, precision=hp, preferred_element_type=F32)

    ones_col = (lax.broadcasted_iota(I32, (L, dh), 1) == 0).astype(BF16)
    m_all = m_ref[...]

    for h in range(H):
        q = qk[:, h * dh:(h + 1) * dh].astype(BF16)
        kf = qk[:, MLSTM_WIDTH + h * dh:MLSTM_WIDTH + (h + 1) * dh] * (dh ** -0.5)
        v_aug = jnp.concatenate([v_ref[:, h * dh:(h + 1) * dh], ones_col], axis=1)

        bc = jnp.broadcast_to(bcum_c[:, H + h:H + h + 1], (L, L))
        li_c = jnp.broadcast_to(gc[:, h:h + 1], (L, L))
        br = bcum_r[H + h:H + h + 1, :]
        li_r = gr[h:h + 1, :]
        m_prev = m_all[h:h + 1, :]

        dmat = jnp.where(tril, bc - br + li_r, NEG_INF)
        inter = bc + m_prev
        m_t = jnp.maximum(inter, jnp.max(dmat, axis=-1, keepdims=True))
        w_ts = jnp.exp(dmat - m_t)
        sc_inter = jnp.exp(inter - m_t)

        sqk = lax.dot_general(q, kf.astype(BF16), NT_DIMS, preferred_element_type=F32) * w_ts
        intra = jnp.dot(sqk.astype(BF16), v_aug, preferred_element_type=F32)
        c_old = c_ref[h]
        cq = jnp.dot(q, c_old.astype(BF16), preferred_element_type=F32)
        num = intra[:, :dh] + sc_inter * cq[:, :dh]
        den = intra[:, dh:dh + 1] + sc_inter[:, 0:1] * cq[:, dh:dh + 1]
        h_t = num / jnp.maximum(jnp.abs(den), jnp.exp(-m_t[:, 0:1]))

        b_last = bc[L - 1:L, :]
        lw = b_last - bc + li_c
        m_new = jnp.maximum(b_last + m_prev, jnp.max(lw, axis=0, keepdims=True))
        w_s = jnp.exp(lw - m_new)
        decay = jnp.exp(b_last + m_prev - m_new)
        kw = (kf * w_s).astype(BF16)
        upd = lax.dot_general(kw, v_aug, TN_DIMS, preferred_element_type=F32)
        c_ref[h] = jnp.concatenate([decay, decay], axis=1) * c_old + upd
        m_ref[h:h + 1, :] = m_new

        hc = _sigmoid(o_ref[:, h * dh:(h + 1) * dh].astype(F32)) * h_t
        y_ref[:, h * dh:(h + 1) * dh] = _rms(hc, ng_ref[:, h * dh:(h + 1) * dh]).astype(BF16)


def _mlstm(mqk, mv, mo, ifc, ifT, conv_w, conv_b, ifb_row, ifb_col, norm_g):
    s = mqk.shape[0]
    L = MLSTM_CHUNK
    row = lambda w: pl.BlockSpec((L, w), lambda i: (i, 0))
    full = lambda a: pl.BlockSpec(a.shape, lambda i: (0,) * a.ndim)
    return pl.pallas_call(
        _mlstm_kernel, grid=(s // L,),
        in_specs=[row(2 * MLSTM_WIDTH), row(MLSTM_WIDTH), row(MLSTM_WIDTH), row(LANES),
                  pl.BlockSpec((SUBLANES, L), lambda i: (0, i)),
                  full(conv_w), full(conv_b), full(ifb_row), full(ifb_col), full(norm_g)],
        out_specs=row(MLSTM_WIDTH),
        out_shape=jax.ShapeDtypeStruct((s, MLSTM_WIDTH), BF16),
        scratch_shapes=[pltpu.VMEM((L + SUBLANES, 2 * MLSTM_WIDTH), F32),
                        pltpu.VMEM((MLSTM_HEADS, MLSTM_HEAD_DIM, 2 * MLSTM_HEAD_DIM), F32),
                        pltpu.VMEM((SUBLANES, L), F32)],
        compiler_params=_cparams("arbitrary"), name="mlstm",
    )(mqk, mv, mo, ifc, ifT, conv_w, conv_b, ifb_row, ifb_col, norm_g)


_R_ROWS = 40


def _merge_kernel(ya_ref, ym_ref, gates_ref, x_ref, wa_ref, wm_ref, wo_ref, bg_ref, fg_ref,
                  wrh_ref, wrl_ref, rb_ref,
                  h1_ref, xn_ref, eid_ref, rank_ref, wcol_ref, cnt_ref, carry_ref):
    tm = TM_MERGE

    @pl.when(pl.program_id(0) == 0)
    def _():
        carry_ref[...] = jnp.zeros(carry_ref.shape, F32)

    gates = _sigmoid(gates_ref[...].astype(F32) + bg_ref[...])
    ua = jnp.dot(ya_ref[...], wa_ref[...], preferred_element_type=F32)
    um = jnp.dot(ym_ref[...], wm_ref[...], preferred_element_type=F32)
    merged = gates[:, :D_MODEL] * ua + gates[:, D_MODEL:] * um
    h1 = x_ref[...] + jnp.dot(merged.astype(BF16), wo_ref[...], preferred_element_type=F32)
    h1_ref[...] = h1
    xn = _rms(h1, fg_ref[...])
    xn_ref[...] = xn

    x_hi = xn.astype(BF16)
    x_lo = (xn - x_hi.astype(F32)).astype(BF16)
    wrh, wrl = wrh_ref[...], wrl_ref[...]
    dot_nt = lambda a, b: lax.dot_general(a, b, NT_DIMS, preferred_element_type=F32)
    lg = dot_nt(wrh, x_hi) + dot_nt(wrh, x_lo) + dot_nt(wrl, x_hi) + rb_ref[:, 0:1]

    G, E = N_GROUPS, EXPERTS_PER_GROUP
    gl = lg[0:G]
    gidx = lax.broadcasted_iota(I32, (G, tm), 0)
    gmax = jnp.max(gl, axis=0, keepdims=True)
    grp = jnp.min(jnp.where(gl == gmax, gidx, G), axis=0, keepdims=True)
    p_grp = 1.0 / jnp.sum(jnp.exp(gl - gmax), axis=0, keepdims=True)
    el = jnp.zeros((E, tm), F32)
    for g in range(G):
        el = jnp.where(grp == g, lg[SUBLANES + g * E:SUBLANES + (g + 1) * E], el)
    ex = jnp.exp(el - jnp.max(el, axis=0, keepdims=True))
    ep = ex / jnp.sum(ex, axis=0, keepdims=True)
    eidx = lax.broadcasted_iota(I32, (E, tm), 0)
    w1 = jnp.max(ep, axis=0, keepdims=True)
    i1 = jnp.min(jnp.where(ep == w1, eidx, E), axis=0, keepdims=True)
    ep2 = jnp.where(eidx == i1, -1.0, ep)
    w2 = jnp.max(ep2, axis=0, keepdims=True)
    i2 = jnp.min(jnp.where(ep2 == w2, eidx, E), axis=0, keepdims=True)
    wsum = w1 + w2
    wt1 = w1 / wsum * p_grp
    wt2 = w2 / wsum * p_grp
    e1 = grp * E + i1
    e2 = grp * E + i2

    xidx = lax.broadcasted_iota(I32, (N_EXPERTS, tm), 0)
    oh1 = xidx == e1
    oh2 = xidx == e2
    member = jnp.where(oh1 | oh2, 1.0, 0.0)
    t_r = lax.broadcasted_iota(I32, (tm, tm), 0)
    t_c = lax.broadcasted_iota(I32, (tm, tm), 1)
    before = (t_r < t_c).astype(BF16)
    cs = jnp.dot(member.astype(BF16), before, preferred_element_type=F32) + carry_ref[:, 0:1]
    r1 = jnp.sum(jnp.where(oh1, cs, 0.0), axis=0, keepdims=True)
    r2 = jnp.sum(jnp.where(oh2, cs, 0.0), axis=0, keepdims=True)
    carry = carry_ref[...] + jnp.sum(member, axis=1, keepdims=True)
    carry_ref[...] = carry
    cnt_ref[...] = carry

    ridx = lax.broadcasted_iota(I32, (SUBLANES, tm), 0)
    eid_ref[...] = jnp.where(ridx == 0, e1, jnp.where(ridx == 1, e2, 0))
    rank_ref[...] = jnp.where(ridx == 0, r1, jnp.where(ridx == 1, r2, 0.0)).astype(I32)
    widx = lax.broadcasted_iota(I32, (LANES, tm), 0)
    wpad = jnp.where(widx == 0, wt1, jnp.where(widx == 1, wt2, 0.0))
    wcol_ref[...] = wpad.T


def _merge(ya, ym, gates, x, wa, wm, wo, bg, fg, wrh, wrl, rb):
    s = x.shape[0]
    tm = TM_MERGE
    row = lambda w: pl.BlockSpec((tm, w), lambda i: (i, 0))
    col = lambda r: pl.BlockSpec((r, tm), lambda i: (0, i))
    full = lambda a: pl.BlockSpec(a.shape, lambda i: (0,) * a.ndim)
    out_shape = (
        jax.ShapeDtypeStruct((s, D_MODEL), F32),
        jax.ShapeDtypeStruct((s, D_MODEL), F32),
        jax.ShapeDtypeStruct((SUBLANES, s), I32),
        jax.ShapeDtypeStruct((SUBLANES, s), I32),
        jax.ShapeDtypeStruct((s, LANES), F32),
        jax.ShapeDtypeStruct((N_EXPERTS, LANES), F32),
    )
    out_specs = (row(D_MODEL), row(D_MODEL), col(SUBLANES), col(SUBLANES), row(LANES),
                 pl.BlockSpec((N_EXPERTS, LANES), lambda i: (0, 0)))
    return pl.pallas_call(
        _merge_kernel, grid=(s // tm,),
        in_specs=[row(ATTN_WIDTH), row(MLSTM_WIDTH), row(2 * D_MODEL), row(D_MODEL),
                  full(wa), full(wm), full(wo), full(bg), full(fg), full(wrh), full(wrl), full(rb)],
        out_specs=out_specs, out_shape=out_shape,
        scratch_shapes=[pltpu.VMEM((N_EXPERTS, LANES), F32)],
        compiler_params=_cparams("arbitrary"), name="merge",
    )(ya, ym, gates, x, wa, wm, wo, bg, fg, wrh, wrl, rb)


def _dest_kernel(eid_ref, rank_ref, poff_ref, dest_ref):
    eid = eid_ref[...]
    xidx = lax.broadcasted_iota(I32, (N_EXPERTS, eid.shape[1]), 0)
    poff = poff_ref[:, 0:1]
    dest_ref[...] = jnp.zeros(dest_ref.shape, I32)
    for kk in range(2):
        base = jnp.sum(jnp.where(xidx == eid[kk:kk + 1, :], poff, 0), axis=0, keepdims=True)
        dest_ref[kk:kk + 1, :] = base + rank_ref[kk:kk + 1, :]


def _dest(eid, rank, poff):
    s = eid.shape[1]
    col = pl.BlockSpec((SUBLANES, IDX_TOK), lambda i: (0, i))
    return pl.pallas_call(
        _dest_kernel, grid=(s // IDX_TOK,),
        in_specs=[col, col, pl.BlockSpec(poff.shape, lambda i: (0, 0))],
        out_specs=col, out_shape=jax.ShapeDtypeStruct((SUBLANES, s), I32),
        compiler_params=_cparams("arbitrary"), name="dest",
    )(eid, rank, poff)


def _load_indices(dest_hbm, idx_ref, sem):
    i = pl.program_id(0)
    per = IDX_TOK // TM_TOK

    @pl.when(i % per == 0)
    def _():
        cols = pl.ds(pl.multiple_of((i // per) * IDX_TOK, IDX_TOK), IDX_TOK)
        cp = pltpu.make_async_copy(dest_hbm.at[:, cols], idx_ref, sem)
        cp.start()
        cp.wait()

    return (i % per) * TM_TOK


def _dispatch_kernel(x_ref, dest_hbm, zeros_hbm, xs_hbm, idx_ref, isem, sem):
    del zeros_hbm
    t0 = _load_indices(dest_hbm, idx_ref, isem)

    def issue(r, c):
        for kk in range(2):
            slot = idx_ref[kk, t0 + r]
            pltpu.make_async_copy(x_ref.at[pl.ds(r, 1)], xs_hbm.at[pl.ds(slot, 1)], sem).start()
        return c

    lax.fori_loop(0, TM_TOK, issue, 0)
    for _ in range(2):
        pltpu.make_async_copy(x_ref, xs_hbm.at[pl.ds(0, TM_TOK)], sem).wait()


def _dispatch(xn, dest, n_pad):
    s = xn.shape[0]
    zeros = jnp.zeros((n_pad, D_MODEL), F32)
    any_spec = pl.BlockSpec(memory_space=pl.ANY)
    return pl.pallas_call(
        _dispatch_kernel, grid=(s // TM_TOK,),
        in_specs=[pl.BlockSpec((TM_TOK, D_MODEL), lambda i: (i, 0)), any_spec, any_spec],
        out_specs=any_spec,
        out_shape=jax.ShapeDtypeStruct((n_pad, D_MODEL), F32),
        scratch_shapes=[pltpu.SMEM((SUBLANES, IDX_TOK), I32),
                        pltpu.SemaphoreType.DMA, pltpu.SemaphoreType.DMA],
        input_output_aliases={2: 0},
        compiler_params=_cparams("arbitrary"), name="dispatch",
    )(xn, dest, zeros)


def _experts_kernel(be_ref, xs_ref, wg_ref, wu_ref, wd_ref, y_ref):
    del be_ref
    xb = xs_ref[...].astype(BF16)
    g = jnp.dot(xb, wg_ref[0].astype(BF16), preferred_element_type=F32)
    u = jnp.dot(xb, wu_ref[0].astype(BF16), preferred_element_type=F32)
    hid = (g * _sigmoid(g) * u).astype(BF16)
    y_ref[...] = jnp.dot(hid, wd_ref[0].astype(BF16), preferred_element_type=F32)


def _experts(blk_e, xs, wg, wu, wd):
    n_pad = xs.shape[0]
    grid_spec = pltpu.PrefetchScalarGridSpec(
        num_scalar_prefetch=1, grid=(n_pad // ROW_BLOCK,),
        in_specs=[pl.BlockSpec((ROW_BLOCK, D_MODEL), lambda b, be: (b, 0)),
                  pl.BlockSpec((1, D_MODEL, EXPERT_FF), lambda b, be: (be[b], 0, 0)),
                  pl.BlockSpec((1, D_MODEL, EXPERT_FF), lambda b, be: (be[b], 0, 0)),
                  pl.BlockSpec((1, EXPERT_FF, D_MODEL), lambda b, be: (be[b], 0, 0))],
        out_specs=pl.BlockSpec((ROW_BLOCK, D_MODEL), lambda b, be: (b, 0)),
    )
    return pl.pallas_call(
        _experts_kernel, grid_spec=grid_spec,
        out_shape=jax.ShapeDtypeStruct((n_pad, D_MODEL), F32),
        compiler_params=_cparams("arbitrary"), name="experts",
    )(blk_e, xs, wg, wu, wd)


def _final_kernel(h1_ref, p_ref, wcol_ref, dest_hbm, yb_hbm, wpg_ref, wpp_ref, pg_ref, fg_ref,
                  out_ref, idx_ref, g0_ref, g1_ref, isem, sem):
    t0 = _load_indices(dest_hbm, idx_ref, isem)
    bufs = (g0_ref, g1_ref)

    def issue(r, c):
        for kk in range(2):
            slot = idx_ref[kk, t0 + r]
            pltpu.make_async_copy(yb_hbm.at[pl.ds(slot, 1)], bufs[kk].at[pl.ds(r, 1)], sem).start()
        return c

    lax.fori_loop(0, TM_TOK, issue, 0)
    for kk in range(2):
        pltpu.make_async_copy(yb_hbm.at[pl.ds(0, TM_TOK)], bufs[kk], sem).wait()

    w = wcol_ref[...]
    h2 = h1_ref[...] + (g0_ref[...] * w[:, 0:1] + g1_ref[...] * w[:, 1:2])
    z = _rms(h2, pg_ref[...]).astype(BF16)
    gate = _sigmoid(jnp.dot(z, wpg_ref[...], preferred_element_type=F32))
    pp = jnp.dot(p_ref[...].astype(BF16), wpp_ref[...], preferred_element_type=F32)
    out_ref[...] = _rms(h2 + pp * gate, fg_ref[...])


def _final(h1, p, wcol, dest, yb, wpg, wpp, pg, fg):
    s = h1.shape[0]
    row = lambda w: pl.BlockSpec((TM_TOK, w), lambda i: (i, 0))
    full = lambda a: pl.BlockSpec(a.shape, lambda i: (0,) * a.ndim)
    any_spec = pl.BlockSpec(memory_space=pl.ANY)
    return pl.pallas_call(
        _final_kernel, grid=(s // TM_TOK,),
        in_specs=[row(D_MODEL), row(PLE_DIM), row(LANES), any_spec, any_spec,
                  full(wpg), full(wpp), full(pg), full(fg)],
        out_specs=row(D_MODEL),
        out_shape=jax.ShapeDtypeStruct((s, D_MODEL), F32),
        scratch_shapes=[pltpu.SMEM((SUBLANES, IDX_TOK), I32),
                        pltpu.VMEM((TM_TOK, D_MODEL), F32), pltpu.VMEM((TM_TOK, D_MODEL), F32),
                        pltpu.SemaphoreType.DMA, pltpu.SemaphoreType.DMA],
        compiler_params=_cparams("arbitrary"), name="final",
    )(h1, p, wcol, dest, yb, wpg, wpp, pg, fg)


def _split_bf16(w):
    hi = w.astype(BF16)
    return hi, (w - hi.astype(F32)).astype(BF16)


def _layer(h, p, mix_norm_g, w_in, b_gate, conv_w, conv_b, mlstm_if_b, mlstm_norm_g,
           w_up_attn, w_up_mlstm, w_out, ffn_norm_g, rg_w, rg_b, re_w, re_b,
           w_gate, w_up, w_down, ple_norm_g, w_ple_gate, w_ple_proj, out_norm_g):
    s = h.shape[0]
    aw, mw, d = ATTN_WIDTH, MLSTM_WIDTH, D_MODEL
    o_mq = 3 * aw
    o_if = o_mq + 4 * mw
    o_g = o_if + 2 * MLSTM_HEADS
    w_if = w_in[:, o_if:o_g]
    wrow = jnp.concatenate(
        [w_in[:, aw:2 * aw], w_in[:, o_mq:o_if], w_in[:, o_g:], w_if,
         jnp.zeros((d, LANES - 2 * MLSTM_HEADS), F32)], axis=1).astype(BF16)
    wt = jnp.concatenate(
        [w_in[:, 0:aw].T, w_in[:, 2 * aw:3 * aw].T, w_if.T,
         jnp.zeros((16 - 2 * MLSTM_HEADS, d), F32)], axis=0).astype(BF16)

    k, kmean, mqk, mv, mo, gates, ifc, qT, vT, ifT = _proj(h, mix_norm_g[None, :], wrow, wt)
    nb = s // MOBA_BLOCK
    ya = _moba(qT, k.reshape(nb, MOBA_BLOCK, aw), vT, kmean.reshape(nb, aw))

    ifb_row = jnp.concatenate([mlstm_if_b, jnp.zeros((LANES - 2 * MLSTM_HEADS,), F32)])[None, :]
    ifb_col = jnp.broadcast_to(mlstm_if_b[:, None], (2 * MLSTM_HEADS, MLSTM_CHUNK))
    ym = _mlstm(mqk, mv, mo, ifc, ifT, conv_w, conv_b[None, :], ifb_row, ifb_col, mlstm_norm_g[None, :])

    wr = jnp.concatenate([rg_w.T, jnp.zeros((SUBLANES - N_GROUPS, d), F32), re_w.T], axis=0)
    wrh, wrl = _split_bf16(wr)
    rb = jnp.concatenate([rg_b, jnp.zeros((SUBLANES - N_GROUPS,), F32), re_b])
    rb = jnp.broadcast_to(rb[:, None], (_R_ROWS, LANES))
    h1, xn, eid, rank, wcol, cnt = _merge(
        ya, ym, gates, h, w_up_attn.astype(BF16), w_up_mlstm.astype(BF16), w_out.astype(BF16),
        b_gate[None, :], ffn_norm_g[None, :], wrh, wrl, rb)

    n_assign = 2 * s
    n_pad = (n_assign + N_EXPERTS * (ROW_BLOCK - 1) + ROW_BLOCK - 1) // ROW_BLOCK * ROW_BLOCK
    counts = cnt[:, 0].astype(I32)
    pcounts = (counts + ROW_BLOCK - 1) // ROW_BLOCK * ROW_BLOCK
    pends = jnp.cumsum(pcounts)
    poffs = pends - pcounts
    blk_start = jnp.arange(n_pad // ROW_BLOCK, dtype=I32) * ROW_BLOCK
    blk_e = jnp.minimum(jnp.sum((pends[None, :] <= blk_start[:, None]).astype(I32), axis=1), N_EXPERTS - 1)
    dest = _dest(eid, rank, jnp.broadcast_to(poffs[:, None], (N_EXPERTS, LANES)))

    xs = _dispatch(xn, dest, n_pad)
    yb = _experts(blk_e, xs, w_gate, w_up, w_down)
    return _final(h1, p, wcol, dest, yb, w_ple_gate.astype(BF16), w_ple_proj.astype(BF16),
                  ple_norm_g[None, :], out_norm_g[None, :])


def kernel(x, p, mix_norm_g, w_in, b_gate, conv_w, conv_b, mlstm_if_b, mlstm_norm_g, w_up_attn, w_up_mlstm, w_out, ffn_norm_g, router_group_w, router_group_b, router_expert_w, router_expert_b, expert_w_gate, expert_w_up, expert_w_down, ple_norm_g, w_ple_gate, w_ple_proj, final_norm_g):
    assert w_in.shape[0] == 1 and x.shape[0] == 1, "one layer, one sequence"
    assert x.shape[1] % IDX_TOK == 0
    out = _layer(x[0], p[0, 0], mix_norm_g[0], w_in[0], b_gate[0], conv_w[0], conv_b[0], mlstm_if_b[0],
                 mlstm_norm_g[0], w_up_attn[0], w_up_mlstm[0], w_out[0], ffn_norm_g[0],
                 router_group_w[0], router_group_b[0], router_expert_w[0], router_expert_b[0],
                 expert_w_gate[0], expert_w_up[0], expert_w_down[0], ple_norm_g[0], w_ple_gate[0],
                 w_ple_proj[0], final_norm_g)
    return out[None]
```

```python
import functools

import jax
import jax.numpy as jnp
from jax import lax
from jax.experimental import pallas as pl
from jax.experimental.pallas import tpu as pltpu

F32 = jnp.float32
BF16 = jnp.bfloat16
I32 = jnp.int32

D_MODEL = 1024
ATTN_HEADS = 8
ATTN_HEAD_DIM = 64
ATTN_WIDTH = ATTN_HEADS * ATTN_HEAD_DIM
MOBA_BLOCK = 256
MOBA_TOPK = 3
KV_GROUP = 4
KV_SPAN = KV_GROUP * MOBA_BLOCK
LOG2E = 1.4426950408889634
MLSTM_HEADS = 4
MLSTM_HEAD_DIM = 128
MLSTM_WIDTH = MLSTM_HEADS * MLSTM_HEAD_DIM
MLSTM_CHUNK = 128
CONV_K = 4
N_GROUPS = 4
EXPERTS_PER_GROUP = 8
N_EXPERTS = N_GROUPS * EXPERTS_PER_GROUP
EXPERT_FF = 512
ROW_BLOCK = 256
PLE_DIM = 256
EPS = 1e-6

LANES = 128
SUBLANES = 8
VMEM_LIMIT = 56 * 1024 * 1024

TM_PROJ = 512
TM_MERGE = 512
TM_TOK = 256
IDX_TOK = 1024

NT_DIMS = (((1,), (1,)), ((), ()))
TN_DIMS = (((0,), (0,)), ((), ()))
NEG_INF = float("-inf")


def _cparams(*sem):
    return pltpu.CompilerParams(dimension_semantics=sem, vmem_limit_bytes=VMEM_LIMIT)


def _rms(x, g):
    return x * lax.rsqrt(jnp.mean(x * x, axis=-1, keepdims=True) + EPS) * g


def _sigmoid(x):
    return 1.0 / (1.0 + jnp.exp(-x))


_C_K, _C_QK, _C_V, _C_O, _C_G, _C_IF, _C_END = 0, 512, 1536, 2048, 2560, 4608, 4736


def _proj_kernel(x_ref, g_ref, wrow_ref, wt_ref,
                 k_ref, kmean_ref, mqk_ref, mv_ref, mo_ref, gates_ref, ifc_ref,
                 qT_ref, vT_ref, ifT_ref):
    xb = _rms(x_ref[...], g_ref[...]).astype(BF16)

    def rowdot(a, b):
        return jnp.dot(xb, wrow_ref[:, a:b], preferred_element_type=F32)

    k = rowdot(_C_K, _C_QK)
    k_ref[...] = k.astype(BF16)
    for b in range(TM_PROJ // MOBA_BLOCK):
        kmean_ref[b] = jnp.mean(k[b * MOBA_BLOCK:(b + 1) * MOBA_BLOCK], axis=0, keepdims=True)
    mqk_ref[...] = rowdot(_C_QK, _C_V).astype(BF16)
    mv_ref[...] = rowdot(_C_V, _C_O).astype(BF16)
    mo_ref[...] = rowdot(_C_O, _C_G).astype(BF16)
    gates_ref[...] = rowdot(_C_G, _C_IF).astype(BF16)
    ifc_ref[...] = rowdot(_C_IF, _C_END)

    def colT(a, b):
        return lax.dot_general(wt_ref[a:b, :], xb, NT_DIMS, preferred_element_type=F32)

    qT = colT(0, ATTN_WIDTH) * (ATTN_HEAD_DIM ** -0.5 * LOG2E)
    for b in range(TM_PROJ // MOBA_BLOCK):
        qT_ref[b] = qT[:, b * MOBA_BLOCK:(b + 1) * MOBA_BLOCK].astype(BF16)
    vT_ref[...] = colT(ATTN_WIDTH, 2 * ATTN_WIDTH).astype(BF16)
    ifT_ref[...] = colT(2 * ATTN_WIDTH, 2 * ATTN_WIDTH + 16)[0:SUBLANES]


def _proj(x, g, wrow, wt):
    s = x.shape[0]
    nb = s // MOBA_BLOCK
    bpt = TM_PROJ // MOBA_BLOCK
    row = lambda w: pl.BlockSpec((TM_PROJ, w), lambda i: (i, 0))
    full = lambda a: pl.BlockSpec(a.shape, lambda i: (0,) * a.ndim)
    out_shape = (
        jax.ShapeDtypeStruct((s, ATTN_WIDTH), BF16),
        jax.ShapeDtypeStruct((nb, 1, ATTN_WIDTH), F32),
        jax.ShapeDtypeStruct((s, 2 * MLSTM_WIDTH), BF16),
        jax.ShapeDtypeStruct((s, MLSTM_WIDTH), BF16),
        jax.ShapeDtypeStruct((s, MLSTM_WIDTH), BF16),
        jax.ShapeDtypeStruct((s, 2 * D_MODEL), BF16),
        jax.ShapeDtypeStruct((s, LANES), F32),
        jax.ShapeDtypeStruct((nb, ATTN_WIDTH, MOBA_BLOCK), BF16),
        jax.ShapeDtypeStruct((ATTN_WIDTH, s), BF16),
        jax.ShapeDtypeStruct((SUBLANES, s), F32),
    )
    out_specs = (
        row(ATTN_WIDTH),
        pl.BlockSpec((bpt, 1, ATTN_WIDTH), lambda i: (i, 0, 0)),
        row(2 * MLSTM_WIDTH), row(MLSTM_WIDTH), row(MLSTM_WIDTH), row(2 * D_MODEL), row(LANES),
        pl.BlockSpec((bpt, ATTN_WIDTH, MOBA_BLOCK), lambda i: (i, 0, 0)),
        pl.BlockSpec((ATTN_WIDTH, TM_PROJ), lambda i: (0, i)),
        pl.BlockSpec((SUBLANES, TM_PROJ), lambda i: (0, i)),
    )
    return pl.pallas_call(
        _proj_kernel, grid=(s // TM_PROJ,),
        in_specs=[row(D_MODEL), full(g), full(wrow), full(wt)],
        out_specs=out_specs, out_shape=out_shape,
        compiler_params=_cparams("arbitrary"), name="proj",
    )(x, g, wrow, wt)


def _moba_kernel(qT_ref, k_ref, vT_ref, km_ref, o_ref, bias_ref, sa_ref, sb_ref):
    i = pl.program_id(1)
    nb = km_ref.shape[0]
    hd = ATTN_HEAD_DIM
    qT = qT_ref[0]
    row = lax.broadcasted_iota(I32, qT.shape, 0)
    km = km_ref[...]
    km_hi = km.astype(BF16)
    km_lo = (km - km_hi.astype(F32)).astype(BF16)
    blk = lax.broadcasted_iota(I32, (nb, MOBA_BLOCK), 0)

    qms = []
    for hh in range(2):
        qm = jnp.where((row >= hh * hd) & (row < (hh + 1) * hd), qT, jnp.zeros_like(qT))
        qms.append(qm)
        gate = (jnp.dot(km_hi, qm, preferred_element_type=F32)
                + jnp.dot(km_lo, qm, preferred_element_type=F32))
        g = jnp.where(blk < i, gate, NEG_INF)
        sel = jnp.zeros(g.shape, F32)
        for _ in range(MOBA_TOPK):
            mx = jnp.max(g, axis=0, keepdims=True)
            first = jnp.min(jnp.where(g == mx, blk, nb), axis=0, keepdims=True)
            pick = (blk == first) & (mx > NEG_INF)
            sel = jnp.where(pick, 1.0, sel)
            g = jnp.where(pick, NEG_INF, g)
        bias_ref[hh] = jnp.where(sel > 0.0, 0.0, NEG_INF)
    q2 = jnp.concatenate(qms, axis=1)

    kpos = lax.broadcasted_iota(I32, (MOBA_BLOCK, MOBA_BLOCK), 0)
    qpos = lax.broadcasted_iota(I32, (MOBA_BLOCK, MOBA_BLOCK), 1)
    causal = kpos <= qpos

    def pv(v, hh, p):
        return jnp.dot(v[hh * hd:(hh + 1) * hd, :], p.astype(BF16), preferred_element_type=F32)

    own_rows = pl.ds(pl.multiple_of((i % KV_GROUP) * MOBA_BLOCK, MOBA_BLOCK), MOBA_BLOCK)
    own_cols = pl.ds(pl.multiple_of(i * MOBA_BLOCK, MOBA_BLOCK), MOBA_BLOCK)
    s_own = jnp.dot(k_ref[i // KV_GROUP, own_rows, :], q2, preferred_element_type=F32)
    v_own = vT_ref[:, own_cols]
    carry = []
    for hh in range(2):
        st = jnp.where(causal, s_own[:, hh * MOBA_BLOCK:(hh + 1) * MOBA_BLOCK], NEG_INF)
        m = jnp.max(st, axis=0, keepdims=True)
        p = jnp.exp2(st - m)
        carry += [m, jnp.sum(p, axis=0, keepdims=True), pv(v_own, hh, p)]

    ng = nb // KV_GROUP

    def score_into(s_ref, jg):
        s_ref[...] = jnp.dot(k_ref[jg], q2, preferred_element_type=F32)

    def fold(s_ref, jg, carry):
        v_g = vT_ref[:, pl.ds(pl.multiple_of(jg * KV_SPAN, KV_SPAN), KV_SPAN)]
        out = []
        for hh in range(2):
            m, l, acc = carry[3 * hh:3 * hh + 3]
            sh = lambda g: s_ref[g * MOBA_BLOCK:(g + 1) * MOBA_BLOCK, hh * MOBA_BLOCK:(hh + 1) * MOBA_BLOCK]
            bias = [bias_ref[hh, pl.ds(jg * KV_GROUP + g, 1), :] for g in range(KV_GROUP)]
            m_new = m
            for g in range(KV_GROUP):
                m_new = jnp.maximum(m_new, jnp.max(sh(g), axis=0, keepdims=True) + bias[g])
            p = jnp.concatenate([jnp.exp2(sh(g) - (m_new - bias[g])) for g in range(KV_GROUP)], axis=0)
            alpha = jnp.exp2(m - m_new)
            out += [m_new, alpha * l + jnp.sum(p, axis=0, keepdims=True), alpha * acc + pv(v_g, hh, p)]
        return tuple(out)

    def body(t, carry):
        score_into(sb_ref, 2 * t + 1)
        carry = fold(sa_ref, 2 * t, carry)
        score_into(sa_ref, jnp.minimum(2 * t + 2, ng - 1))
        return fold(sb_ref, 2 * t + 1, carry)

    n_groups = (i + KV_GROUP - 1) // KV_GROUP
    score_into(sa_ref, 0)
    carry = lax.fori_loop(0, (n_groups + 1) // 2, body, tuple(carry))
    oT = jnp.concatenate([carry[2] / carry[1], carry[5] / carry[4]], axis=0)
    o_ref[...] = oT.T.astype(BF16)


def _moba(qT, k, vT, km):
    s = k.shape[0]
    nb = s // MOBA_BLOCK
    assert nb % (2 * KV_GROUP) == 0
    ng = nb // KV_GROUP
    pw = 2 * ATTN_HEAD_DIM
    return pl.pallas_call(
        _moba_kernel, grid=(ATTN_HEADS // 2, nb),
        in_specs=[
            pl.BlockSpec((1, pw, MOBA_BLOCK), lambda p, i: (i, p, 0)),
            pl.BlockSpec((ng, KV_SPAN, pw), lambda p, i: (0, 0, p)),
            pl.BlockSpec((pw, s), lambda p, i: (p, 0)),
            pl.BlockSpec((nb, pw), lambda p, i: (0, p)),
        ],
        out_specs=pl.BlockSpec((MOBA_BLOCK, pw), lambda p, i: (i, p)),
        out_shape=jax.ShapeDtypeStruct((s, ATTN_WIDTH), BF16),
        scratch_shapes=[pltpu.VMEM((2, nb, MOBA_BLOCK), F32),
                        pltpu.VMEM((KV_SPAN, 2 * MOBA_BLOCK), F32), pltpu.VMEM((KV_SPAN, 2 * MOBA_BLOCK), F32)],
        compiler_params=_cparams("arbitrary", "arbitrary"), name="moba",
    )(qT, k.reshape(ng, KV_SPAN, ATTN_WIDTH), vT, km)


def _log_sigmoid(x):
    return -(jnp.maximum(-x, 0.0) + jnp.log1p(jnp.exp(-jnp.abs(x))))


def _mlstm_kernel(qk_ref, v_ref, o_ref, ifc_ref, ifT_ref, cw_ref, cb_ref, ifbr_ref, ifbc_ref, ng_ref,
                  y_ref, ext_ref, c_ref, m_ref):
    L, H, dh = MLSTM_CHUNK, MLSTM_HEADS, MLSTM_HEAD_DIM
    halo = SUBLANES

    @pl.when(pl.program_id(0) == 0)
    def _():
        ext_ref[0:halo, :] = jnp.zeros((halo, 2 * MLSTM_WIDTH), F32)
        c_ref[...] = jnp.zeros(c_ref.shape, F32)
        m_ref[...] = jnp.zeros(m_ref.shape, F32)

    ext_ref[halo:halo + L, :] = qk_ref[...].astype(F32)
    y = cb_ref[...]
    for kk in range(CONV_K):
        off = halo - (CONV_K - 1) + kk
        y = y + ext_ref[off:off + L, :] * cw_ref[kk:kk + 1, :]
    ext_ref[0:halo, :] = ext_ref[L:L + halo, :]
    qk = y * _sigmoid(y)

    r_i = lax.broadcasted_iota(I32, (L, L), 0)
    c_i = lax.broadcasted_iota(I32, (L, L), 1)
    tril = (c_i <= r_i)
    tril_f = tril.astype(F32)
    triu_f = (r_i <= c_i).astype(F32)
    hp = lax.Precision.HIGHEST

    gc = ifc_ref[...] + ifbr_ref[...]
    gr = ifT_ref[...] + ifbc_ref[...]
    bcum_c = jnp.dot(tril_f, _log_sigmoid(gc), precision=hp, preferred_element_type=F32)
    bcum_r = jnp.dot(_log_sigmoid(gr), triu_f, precision=hp, preferred_element_type=F32)

    ones_col = (lax.broadcasted_iota(I32, (L, dh), 1) == 0).astype(BF16)
    m_all = m_ref[...]

    for h in range(H):
        q = qk[:, h * dh:(h + 1) * dh].astype(BF16)
        kf = qk[:, MLSTM_WIDTH + h * dh:MLSTM_WIDTH + (h + 1) * dh] * (dh ** -0.5)
        v_aug = jnp.concatenate([v_ref[:, h * dh:(h + 1) * dh], ones_col], axis=1)

        bc = jnp.broadcast_to(bcum_c[:, H + h:H + h + 1], (L, L))
        li_c = jnp.broadcast_to(gc[:, h:h + 1], (L, L))
        br = bcum_r[H + h:H + h + 1, :]
        li_r = gr[h:h + 1, :]
        m_prev = m_all[h:h + 1, :]

        dmat = jnp.where(tril, bc - br + li_r, NEG_INF)
        inter = bc + m_prev
        m_t = jnp.maximum(inter, jnp.max(dmat, axis=-1, keepdims=True))
        w_ts = jnp.exp(dmat - m_t)
        sc_inter = jnp.exp(inter - m_t)

        sqk = lax.dot_general(q, kf.astype(BF16), NT_DIMS, preferred_element_type=F32) * w_ts
        intra = jnp.dot(sqk.astype(BF16), v_aug, preferred_element_type=F32)
        c_old = c_ref[h]
        cq = jnp.dot(q, c_old.astype(BF16), preferred_element_type=F32)
        num = intra[:, :dh] + sc_inter * cq[:, :dh]
        den = intra[:, dh:dh + 1] + sc_inter[:, 0:1] * cq[:, dh:dh + 1]
        h_t = num / jnp.maximum(jnp.abs(den), jnp.exp(-m_t[:, 0:1]))

        b_last = bc[L - 1:L, :]
        lw = b_last - bc + li_c
        m_new = jnp.maximum(b_last + m_prev, jnp.max(lw, axis=0, keepdims=True))
        w_s = jnp.exp(lw - m_new)
        decay = jnp.exp(b_last + m_prev - m_new)
        kw = (kf * w_s).astype(BF16)
        upd = lax.dot_general(kw, v_aug, TN_DIMS, preferred_element_type=F32)
        c_ref[h] = jnp.concatenate([decay, decay], axis=1) * c_old + upd
        m_ref[h:h + 1, :] = m_new

        hc = _sigmoid(o_ref[:, h * dh:(h + 1) * dh].astype(F32)) * h_t
        y_ref[:, h * dh:(h + 1) * dh] = _rms(hc, ng_ref[:, h * dh:(h + 1) * dh]).astype(BF16)


def _mlstm(mqk, mv, mo, ifc, ifT, conv_w, conv_b, ifb_row, ifb_col, norm_g):
    s = mqk.shape[0]
    L = MLSTM_CHUNK
    row = lambda w: pl.BlockSpec((L, w), lambda i: (i, 0))
    full = lambda a: pl.BlockSpec(a.shape, lambda i: (0,) * a.ndim)
    return pl.pallas_call(
        _mlstm_kernel, grid=(s // L,),
        in_specs=[row(2 * MLSTM_WIDTH), row(MLSTM_WIDTH), row(MLSTM_WIDTH), row(LANES),
                  pl.BlockSpec((SUBLANES, L), lambda i: (0, i)),
                  full(conv_w), full(conv_b), full(ifb_row), full(ifb_col), full(norm_g)],
        out_specs=row(MLSTM_WIDTH),
        out_shape=jax.ShapeDtypeStruct((s, MLSTM_WIDTH), BF16),
        scratch_shapes=[pltpu.VMEM((L + SUBLANES, 2 * MLSTM_WIDTH), F32),
                        pltpu.VMEM((MLSTM_HEADS, MLSTM_HEAD_DIM, 2 * MLSTM_HEAD_DIM), F32),
                        pltpu.VMEM((SUBLANES, L), F32)],
        compiler_params=_cparams("arbitrary"), name="mlstm",
    )(mqk, mv, mo, ifc, ifT, conv_w, conv_b, ifb_row, ifb_col, norm_g)


_R_ROWS = 40


def _merge_kernel(ya_ref, ym_ref, gates_ref, x_ref, wa_ref, wm_ref, wo_ref, bg_ref, fg_ref,
                  wrh_ref, wrl_ref, rb_ref,
                  h1_ref, xn_ref, eid_ref, rank_ref, wcol_ref, cnt_ref, carry_ref):
    tm = TM_MERGE

    @pl.when(pl.program_id(0) == 0)
    def _():
        carry_ref[...] = jnp.zeros(carry_ref.shape, F32)

    gates = _sigmoid(gates_ref[...].astype(F32) + bg_ref[...])
    ua = jnp.dot(ya_ref[...], wa_ref[...], preferred_element_type=F32)
    um = jnp.dot(ym_ref[...], wm_ref[...], preferred_element_type=F32)
    merged = gates[:, :D_MODEL] * ua + gates[:, D_MODEL:] * um
    h1 = x_ref[...] + jnp.dot(merged.astype(BF16), wo_ref[...], preferred_element_type=F32)
    h1_ref[...] = h1
    xn = _rms(h1, fg_ref[...])
    xn_ref[...] = xn

    x_hi = xn.astype(BF16)
    x_lo = (xn - x_hi.astype(F32)).astype(BF16)
    wrh, wrl = wrh_ref[...], wrl_ref[...]
    dot_nt = lambda a, b: lax.dot_general(a, b, NT_DIMS, preferred_element_type=F32)
    lg = dot_nt(wrh, x_hi) + dot_nt(wrh, x_lo) + dot_nt(wrl, x_hi) + rb_ref[:, 0:1]

    G, E = N_GROUPS, EXPERTS_PER_GROUP
    gl = lg[0:G]
    gidx = lax.broadcasted_iota(I32, (G, tm), 0)
    gmax = jnp.max(gl, axis=0, keepdims=True)
    grp = jnp.min(jnp.where(gl == gmax, gidx, G), axis=0, keepdims=True)
    p_grp = 1.0 / jnp.sum(jnp.exp(gl - gmax), axis=0, keepdims=True)
    el = jnp.zeros((E, tm), F32)
    for g in range(G):
        el = jnp.where(grp == g, lg[SUBLANES + g * E:SUBLANES + (g + 1) * E], el)
    ex = jnp.exp(el - jnp.max(el, axis=0, keepdims=True))
    ep = ex / jnp.sum(ex, axis=0, keepdims=True)
    eidx = lax.broadcasted_iota(I32, (E, tm), 0)
    w1 = jnp.max(ep, axis=0, keepdims=True)
    i1 = jnp.min(jnp.where(ep == w1, eidx, E), axis=0, keepdims=True)
    ep2 = jnp.where(eidx == i1, -1.0, ep)
    w2 = jnp.max(ep2, axis=0, keepdims=True)
    i2 = jnp.min(jnp.where(ep2 == w2, eidx, E), axis=0, keepdims=True)
    wsum = w1 + w2
    wt1 = w1 / wsum * p_grp
    wt2 = w2 / wsum * p_grp
    e1 = grp * E + i1
    e2 = grp * E + i2

    xidx = lax.broadcasted_iota(I32, (N_EXPERTS, tm), 0)
    oh1 = xidx == e1
    oh2 = xidx == e2
    member = jnp.where(oh1 | oh2, 1.0, 0.0)
    t_r = lax.broadcasted_iota(I32, (tm, tm), 0)
    t_c = lax.broadcasted_iota(I32, (tm, tm), 1)
    before = (t_r < t_c).astype(BF16)
    cs = jnp.dot(member.astype(BF16), before, preferred_element_type=F32) + carry_ref[:, 0:1]
    r1 = jnp.sum(jnp.where(oh1, cs, 0.0), axis=0, keepdims=True)
    r2 = jnp.sum(jnp.where(oh2, cs, 0.0), axis=0, keepdims=True)
    carry = carry_ref[...] + jnp.sum(member, axis=1, keepdims=True)
    carry_ref[...] = carry
    cnt_ref[...] = carry

    ridx = lax.broadcasted_iota(I32, (SUBLANES, tm), 0)
    eid_ref[...] = jnp.where(ridx == 0, e1, jnp.where(ridx == 1, e2, 0))
    rank_ref[...] = jnp.where(ridx == 0, r1, jnp.where(ridx == 1, r2, 0.0)).astype(I32)
    widx = lax.broadcasted_iota(I32, (LANES, tm), 0)
    wpad = jnp.where(widx == 0, wt1, jnp.where(widx == 1, wt2, 0.0))
    wcol_ref[...] = wpad.T


def _merge(ya, ym, gates, x, wa, wm, wo, bg, fg, wrh, wrl, rb):
    s = x.shape[0]
    tm = TM_MERGE
    row = lambda w: pl.BlockSpec((tm, w), lambda i: (i, 0))
    col = lambda r: pl.BlockSpec((r, tm), lambda i: (0, i))
    full = lambda a: pl.BlockSpec(a.shape, lambda i: (0,) * a.ndim)
    out_shape = (
        jax.ShapeDtypeStruct((s, D_MODEL), F32),
        jax.ShapeDtypeStruct((s, D_MODEL), F32),
        jax.ShapeDtypeStruct((SUBLANES, s), I32),
        jax.ShapeDtypeStruct((SUBLANES, s), I32),
        jax.ShapeDtypeStruct((s, LANES), F32),
        jax.ShapeDtypeStruct((N_EXPERTS, LANES), F32),
    )
    out_specs = (row(D_MODEL), row(D_MODEL), col(SUBLANES), col(SUBLANES), row(LANES),
                 pl.BlockSpec((N_EXPERTS, LANES), lambda i: (0, 0)))
    return pl.pallas_call(
        _merge_kernel, grid=(s // tm,),
        in_specs=[row(ATTN_WIDTH), row(MLSTM_WIDTH), row(2 * D_MODEL), row(D_MODEL),
                  full(wa), full(wm), full(wo), full(bg), full(fg), full(wrh), full(wrl), full(rb)],
        out_specs=out_specs, out_shape=out_shape,
        scratch_shapes=[pltpu.VMEM((N_EXPERTS, LANES), F32)],
        compiler_params=_cparams("arbitrary"), name="merge",
    )(ya, ym, gates, x, wa, wm, wo, bg, fg, wrh, wrl, rb)


def _dest_kernel(eid_ref, rank_ref, poff_ref, dest_ref):
    eid = eid_ref[...]
    xidx = lax.broadcasted_iota(I32, (N_EXPERTS, eid.shape[1]), 0)
    poff = poff_ref[:, 0:1]
    dest_ref[...] = jnp.zeros(dest_ref.shape, I32)
    for kk in range(2):
        base = jnp.sum(jnp.where(xidx == eid[kk:kk + 1, :], poff, 0), axis=0, keepdims=True)
        dest_ref[kk:kk + 1, :] = base + rank_ref[kk:kk + 1, :]


def _dest(eid, rank, poff):
    s = eid.shape[1]
    col = pl.BlockSpec((SUBLANES, IDX_TOK), lambda i: (0, i))
    return pl.pallas_call(
        _dest_kernel, grid=(s // IDX_TOK,),
        in_specs=[col, col, pl.BlockSpec(poff.shape, lambda i: (0, 0))],
        out_specs=col, out_shape=jax.ShapeDtypeStruct((SUBLANES, s), I32),
        compiler_params=_cparams("arbitrary"), name="dest",
    )(eid, rank, poff)


def _load_indices(dest_hbm, idx_ref, sem):
    i = pl.program_id(0)
    per = IDX_TOK // TM_TOK

    @pl.when(i % per == 0)
    def _():
        cols = pl.ds(pl.multiple_of((i // per) * IDX_TOK, IDX_TOK), IDX_TOK)
        cp = pltpu.make_async_copy(dest_hbm.at[:, cols], idx_ref, sem)
        cp.start()
        cp.wait()

    return (i % per) * TM_TOK


def _dispatch_kernel(x_ref, dest_hbm, zeros_hbm, xs_hbm, idx_ref, isem, sem):
    del zeros_hbm
    t0 = _load_indices(dest_hbm, idx_ref, isem)

    def issue(r, c):
        for kk in range(2):
            slot = idx_ref[kk, t0 + r]
            pltpu.make_async_copy(x_ref.at[pl.ds(r, 1)], xs_hbm.at[pl.ds(slot, 1)], sem).start()
        return c

    lax.fori_loop(0, TM_TOK, issue, 0)
    for _ in range(2):
        pltpu.make_async_copy(x_ref, xs_hbm.at[pl.ds(0, TM_TOK)], sem).wait()


def _dispatch(xn, dest, n_pad):
    s = xn.shape[0]
    zeros = jnp.zeros((n_pad, D_MODEL), F32)
    any_spec = pl.BlockSpec(memory_space=pl.ANY)
    return pl.pallas_call(
        _dispatch_kernel, grid=(s // TM_TOK,),
        in_specs=[pl.BlockSpec((TM_TOK, D_MODEL), lambda i: (i, 0)), any_spec, any_spec],
        out_specs=any_spec,
        out_shape=jax.ShapeDtypeStruct((n_pad, D_MODEL), F32),
        scratch_shapes=[pltpu.SMEM((SUBLANES, IDX_TOK), I32),
                        pltpu.SemaphoreType.DMA, pltpu.SemaphoreType.DMA],
        input_output_aliases={2: 0},
        compiler_params=_cparams("arbitrary"), name="dispatch",
    )(xn, dest, zeros)


def _experts_kernel(be_ref, xs_ref, wg_ref, wu_ref, wd_ref, y_ref):
    del be_ref
    xb = xs_ref[...].astype(BF16)
    g = jnp.dot(xb, wg_ref[0].astype(BF16), preferred_element_type=F32)
    u = jnp.dot(xb, wu_ref[0].astype(BF16), preferred_element_type=F32)
    hid = (g * _sigmoid(g) * u).astype(BF16)
    y_ref[...] = jnp.dot(hid, wd_ref[0].astype(BF16), preferred_element_type=F32)


def _experts(blk_e, xs, wg, wu, wd):
    n_pad = xs.shape[0]
    grid_spec = pltpu.PrefetchScalarGridSpec(
        num_scalar_prefetch=1, grid=(n_pad // ROW_BLOCK,),
        in_specs=[pl.BlockSpec((ROW_BLOCK, D_MODEL), lambda b, be: (b, 0)),
                  pl.BlockSpec((1, D_MODEL, EXPERT_FF), lambda b, be: (be[b], 0, 0)),
                  pl.BlockSpec((1, D_MODEL, EXPERT_FF), lambda b, be: (be[b], 0, 0)),
                  pl.BlockSpec((1, EXPERT_FF, D_MODEL), lambda b, be: (be[b], 0, 0))],
        out_specs=pl.BlockSpec((ROW_BLOCK, D_MODEL), lambda b, be: (b, 0)),
    )
    return pl.pallas_call(
        _experts_kernel, grid_spec=grid_spec,
        out_shape=jax.ShapeDtypeStruct((n_pad, D_MODEL), F32),
        compiler_params=_cparams("arbitrary"), name="experts",
    )(blk_e, xs, wg, wu, wd)


def _final_kernel(h1_ref, p_ref, wcol_ref, dest_hbm, yb_hbm, wpg_ref, wpp_ref, pg_ref, fg_ref,
                  out_ref, idx_ref, g0_ref, g1_ref, isem, sem):
    t0 = _load_indices(dest_hbm, idx_ref, isem)
    bufs = (g0_ref, g1_ref)

    def issue(r, c):
        for kk in range(2):
            slot = idx_ref[kk, t0 + r]
            pltpu.make_async_copy(yb_hbm.at[pl.ds(slot, 1)], bufs[kk].at[pl.ds(r, 1)], sem).start()
        return c

    lax.fori_loop(0, TM_TOK, issue, 0)
    for kk in range(2):
        pltpu.make_async_copy(yb_hbm.at[pl.ds(0, TM_TOK)], bufs[kk], sem).wait()

    w = wcol_ref[...]
    h2 = h1_ref[...] + (g0_ref[...] * w[:, 0:1] + g1_ref[...] * w[:, 1:2])
    z = _rms(h2, pg_ref[...]).astype(BF16)
    gate = _sigmoid(jnp.dot(z, wpg_ref[...], preferred_element_type=F32))
    pp = jnp.dot(p_ref[...].astype(BF16), wpp_ref[...], preferred_element_type=F32)
    out_ref[...] = _rms(h2 + pp * gate, fg_ref[...])


def _final(h1, p, wcol, dest, yb, wpg, wpp, pg, fg):
    s = h1.shape[0]
    row = lambda w: pl.BlockSpec((TM_TOK, w), lambda i: (i, 0))
    full = lambda a: pl.BlockSpec(a.shape, lambda i: (0,) * a.ndim)
    any_spec = pl.BlockSpec(memory_space=pl.ANY)
    return pl.pallas_call(
        _final_kernel, grid=(s // TM_TOK,),
        in_specs=[row(D_MODEL), row(PLE_DIM), row(LANES), any_spec, any_spec,
                  full(wpg), full(wpp), full(pg), full(fg)],
        out_specs=row(D_MODEL),
        out_shape=jax.ShapeDtypeStruct((s, D_MODEL), F32),
        scratch_shapes=[pltpu.SMEM((SUBLANES, IDX_TOK), I32),
                        pltpu.VMEM((TM_TOK, D_MODEL), F32), pltpu.VMEM((TM_TOK, D_MODEL), F32),
                        pltpu.SemaphoreType.DMA, pltpu.SemaphoreType.DMA],
        compiler_params=_cparams("arbitrary"), name="final",
    )(h1, p, wcol, dest, yb, wpg, wpp, pg, fg)


def _split_bf16(w):
    hi = w.astype(BF16)
    return hi, (w - hi.astype(F32)).astype(BF16)


def _layer(h, p, mix_norm_g, w_in, b_gate, conv_w, conv_b, mlstm_if_b, mlstm_norm_g,
           w_up_attn, w_up_mlstm, w_out, ffn_norm_g, rg_w, rg_b, re_w, re_b,
           w_gate, w_up, w_down, ple_norm_g, w_ple_gate, w_ple_proj, out_norm_g):
    s = h.shape[0]
    aw, mw, d = ATTN_WIDTH, MLSTM_WIDTH, D_MODEL
    o_mq = 3 * aw
    o_if = o_mq + 4 * mw
    o_g = o_if + 2 * MLSTM_HEADS
    w_if = w_in[:, o_if:o_g]
    wrow = jnp.concatenate(
        [w_in[:, aw:2 * aw], w_in[:, o_mq:o_if], w_in[:, o_g:], w_if,
         jnp.zeros((d, LANES - 2 * MLSTM_HEADS), F32)], axis=1).astype(BF16)
    wt = jnp.concatenate(
        [w_in[:, 0:aw].T, w_in[:, 2 * aw:3 * aw].T, w_if.T,
         jnp.zeros((16 - 2 * MLSTM_HEADS, d), F32)], axis=0).astype(BF16)

    k, kmean, mqk, mv, mo, gates, ifc, qT, vT, ifT = _proj(h, mix_norm_g[None, :], wrow, wt)
    nb = s // MOBA_BLOCK
    ya = _moba(qT, k, vT, kmean.reshape(nb, aw))

    ifb_row = jnp.concatenate([mlstm_if_b, jnp.zeros((LANES - 2 * MLSTM_HEADS,), F32)])[None, :]
    ifb_col = jnp.broadcast_to(mlstm_if_b[:, None], (2 * MLSTM_HEADS, MLSTM_CHUNK))
    ym = _mlstm(mqk, mv, mo, ifc, ifT, conv_w, conv_b[None, :], ifb_row, ifb_col, mlstm_norm_g[None, :])

    wr = jnp.concatenate([rg_w.T, jnp.zeros((SUBLANES - N_GROUPS, d), F32), re_w.T], axis=0)
    wrh, wrl = _split_bf16(wr)
    rb = jnp.concatenate([rg_b, jnp.zeros((SUBLANES - N_GROUPS,), F32), re_b])
    rb = jnp.broadcast_to(rb[:, None], (_R_ROWS, LANES))
    h1, xn, eid, rank, wcol, cnt = _merge(
        ya, ym, gates, h, w_up_attn.astype(BF16), w_up_mlstm.astype(BF16), w_out.astype(BF16),
        b_gate[None, :], ffn_norm_g[None, :], wrh, wrl, rb)

    n_assign = 2 * s
    n_pad = (n_assign + N_EXPERTS * (ROW_BLOCK - 1) + ROW_BLOCK - 1) // ROW_BLOCK * ROW_BLOCK
    counts = cnt[:, 0].astype(I32)
    pcounts = (counts + ROW_BLOCK - 1) // ROW_BLOCK * ROW_BLOCK
    pends = jnp.cumsum(pcounts)
    poffs = pends - pcounts
    blk_start = jnp.arange(n_pad // ROW_BLOCK, dtype=I32) * ROW_BLOCK
    blk_e = jnp.minimum(jnp.sum((pends[None, :] <= blk_start[:, None]).astype(I32), axis=1), N_EXPERTS - 1)
    dest = _dest(eid, rank, jnp.broadcast_to(poffs[:, None], (N_EXPERTS, LANES)))

    xs = _dispatch(xn, dest, n_pad)
    yb = _experts(blk_e, xs, w_gate, w_up, w_down)
    return _final(h1, p, wcol, dest, yb, w_ple_gate.astype(BF16), w_ple_proj.astype(BF16),
                  ple_norm_g[None, :], out_norm_g[None, :])


def kernel(x, p, mix_norm_g, w_in, b_gate, conv_w, conv_b, mlstm_if_b, mlstm_norm_g, w_up_attn, w_up_mlstm, w_out, ffn_norm_g, router_group_w, router_group_b, router_expert_w, router_expert_b, expert_w_gate, expert_w_up, expert_w_down, ple_norm_g, w_ple_gate, w_ple_proj, final_norm_g):
    assert w_in.shape[0] == 1 and x.shape[0] == 1, "one layer, one sequence"
    assert x.shape[1] % IDX_TOK == 0
    out = _layer(x[0], p[0, 0], mix_norm_g[0], w_in[0], b_gate[0], conv_w[0], conv_b[0], mlstm_if_b[0],
                 mlstm_norm_g[0], w_up_attn[0], w_up_mlstm[0], w_out[0], ffn_norm_g[0],
                 router_group_w[0], router_group_b[0], router_expert_w[0], router_expert_b[0],
                 expert_w_gate[0], expert_w_up[0], expert_w_down[0], ple_norm_g[0], w_ple_gate[0],
                 w_ple_proj[0], final_norm_g)
    return out[None]
```

```python
import functools

import jax
import jax.numpy as jnp
from jax import lax
from jax.experimental import pallas as pl
from jax.experimental.pallas import tpu as pltpu

F32 = jnp.float32
BF16 = jnp.bfloat16
I32 = jnp.int32

D_MODEL = 1024
ATTN_HEADS = 8
ATTN_HEAD_DIM = 64
ATTN_WIDTH = ATTN_HEADS * ATTN_HEAD_DIM
MOBA_BLOCK = 256
MOBA_TOPK = 3
KV_GROUP = 2
KV_SPAN = KV_GROUP * MOBA_BLOCK
LOG2E = 1.4426950408889634
MLSTM_HEADS = 4
MLSTM_HEAD_DIM = 128
MLSTM_WIDTH = MLSTM_HEADS * MLSTM_HEAD_DIM
MLSTM_CHUNK = 128
CONV_K = 4
N_GROUPS = 4
EXPERTS_PER_GROUP = 8
N_EXPERTS = N_GROUPS * EXPERTS_PER_GROUP
EXPERT_FF = 512
ROW_BLOCK = 256
PLE_DIM = 256
EPS = 1e-6

LANES = 128
SUBLANES = 8
VMEM_LIMIT = 56 * 1024 * 1024

TM_PROJ = 512
TM_MERGE = 512
TM_TOK = 256
IDX_TOK = 1024
ISSUE_UNROLL = 8

NT_DIMS = (((1,), (1,)), ((), ()))
TN_DIMS = (((0,), (0,)), ((), ()))
NEG_INF = float("-inf")


def _cparams(*sem):
    return pltpu.CompilerParams(dimension_semantics=sem, vmem_limit_bytes=VMEM_LIMIT)


def _rms(x, g):
    return x * lax.rsqrt(jnp.mean(x * x, axis=-1, keepdims=True) + EPS) * g


def _sigmoid(x):
    return 1.0 / (1.0 + jnp.exp(-x))


_C_K, _C_QK, _C_V, _C_O, _C_G, _C_IF, _C_END = 0, 512, 1536, 2048, 2560, 4608, 4736


def _proj_kernel(x_ref, g_ref, wrow_ref, wt_ref,
                 k_ref, kmean_ref, mqk_ref, mv_ref, mo_ref, gates_ref, ifc_ref,
                 qT_ref, vT_ref, ifT_ref):
    xb = _rms(x_ref[...], g_ref[...]).astype(BF16)

    def rowdot(a, b):
        return jnp.dot(xb, wrow_ref[:, a:b], preferred_element_type=F32)

    k = rowdot(_C_K, _C_QK)
    k_ref[...] = k.astype(BF16)
    for b in range(TM_PROJ // MOBA_BLOCK):
        kmean_ref[b] = jnp.mean(k[b * MOBA_BLOCK:(b + 1) * MOBA_BLOCK], axis=0, keepdims=True)
    mqk_ref[...] = rowdot(_C_QK, _C_V).astype(BF16)
    mv_ref[...] = rowdot(_C_V, _C_O).astype(BF16)
    mo_ref[...] = rowdot(_C_O, _C_G).astype(BF16)
    gates_ref[...] = rowdot(_C_G, _C_IF).astype(BF16)
    ifc_ref[...] = rowdot(_C_IF, _C_END)

    def colT(a, b):
        return lax.dot_general(wt_ref[a:b, :], xb, NT_DIMS, preferred_element_type=F32)

    qT = colT(0, ATTN_WIDTH) * (ATTN_HEAD_DIM ** -0.5 * LOG2E)
    for b in range(TM_PROJ // MOBA_BLOCK):
        qT_ref[b] = qT[:, b * MOBA_BLOCK:(b + 1) * MOBA_BLOCK].astype(BF16)
    vT_ref[...] = colT(ATTN_WIDTH, 2 * ATTN_WIDTH).astype(BF16)
    ifT_ref[...] = colT(2 * ATTN_WIDTH, 2 * ATTN_WIDTH + 16)[0:SUBLANES]


def _proj(x, g, wrow, wt):
    s = x.shape[0]
    nb = s // MOBA_BLOCK
    bpt = TM_PROJ // MOBA_BLOCK
    row = lambda w: pl.BlockSpec((TM_PROJ, w), lambda i: (i, 0))
    full = lambda a: pl.BlockSpec(a.shape, lambda i: (0,) * a.ndim)
    out_shape = (
        jax.ShapeDtypeStruct((s, ATTN_WIDTH), BF16),
        jax.ShapeDtypeStruct((nb, 1, ATTN_WIDTH), F32),
        jax.ShapeDtypeStruct((s, 2 * MLSTM_WIDTH), BF16),
        jax.ShapeDtypeStruct((s, MLSTM_WIDTH), BF16),
        jax.ShapeDtypeStruct((s, MLSTM_WIDTH), BF16),
        jax.ShapeDtypeStruct((s, 2 * D_MODEL), BF16),
        jax.ShapeDtypeStruct((s, LANES), F32),
        jax.ShapeDtypeStruct((nb, ATTN_WIDTH, MOBA_BLOCK), BF16),
        jax.ShapeDtypeStruct((ATTN_WIDTH, s), BF16),
        jax.ShapeDtypeStruct((SUBLANES, s), F32),
    )
    out_specs = (
        row(ATTN_WIDTH),
        pl.BlockSpec((bpt, 1, ATTN_WIDTH), lambda i: (i, 0, 0)),
        row(2 * MLSTM_WIDTH), row(MLSTM_WIDTH), row(MLSTM_WIDTH), row(2 * D_MODEL), row(LANES),
        pl.BlockSpec((bpt, ATTN_WIDTH, MOBA_BLOCK), lambda i: (i, 0, 0)),
        pl.BlockSpec((ATTN_WIDTH, TM_PROJ), lambda i: (0, i)),
        pl.BlockSpec((SUBLANES, TM_PROJ), lambda i: (0, i)),
    )
    return pl.pallas_call(
        _proj_kernel, grid=(s // TM_PROJ,),
        in_specs=[row(D_MODEL), full(g), full(wrow), full(wt)],
        out_specs=out_specs, out_shape=out_shape,
        compiler_params=_cparams("arbitrary"), name="proj",
    )(x, g, wrow, wt)


def _moba_kernel(qT_ref, k_ref, vT_ref, km_ref, o_ref, bias_ref, sa_ref, sb_ref):
    i = pl.program_id(1)
    nb = km_ref.shape[0]
    hd = ATTN_HEAD_DIM
    qT = qT_ref[0]
    row = lax.broadcasted_iota(I32, qT.shape, 0)
    km = km_ref[...]
    km_hi = km.astype(BF16)
    km_lo = (km - km_hi.astype(F32)).astype(BF16)
    blk = lax.broadcasted_iota(I32, (nb, MOBA_BLOCK), 0)

    qms = []
    for hh in range(2):
        qm = jnp.where((row >= hh * hd) & (row < (hh + 1) * hd), qT, jnp.zeros_like(qT))
        qms.append(qm)
        gate = (jnp.dot(km_hi, qm, preferred_element_type=F32)
                + jnp.dot(km_lo, qm, preferred_element_type=F32))
        g = jnp.where(blk < i, gate, NEG_INF)
        sel = jnp.zeros(g.shape, F32)
        for _ in range(MOBA_TOPK):
            mx = jnp.max(g, axis=0, keepdims=True)
            first = jnp.min(jnp.where(g == mx, blk, nb), axis=0, keepdims=True)
            pick = (blk == first) & (mx > NEG_INF)
            sel = jnp.where(pick, 1.0, sel)
            g = jnp.where(pick, NEG_INF, g)
        bias_ref[hh] = jnp.where(sel > 0.0, 0.0, NEG_INF)
    q2 = jnp.concatenate(qms, axis=1)

    kpos = lax.broadcasted_iota(I32, (MOBA_BLOCK, MOBA_BLOCK), 0)
    qpos = lax.broadcasted_iota(I32, (MOBA_BLOCK, MOBA_BLOCK), 1)
    causal = kpos <= qpos

    def pv(v, hh, p):
        return jnp.dot(v[hh * hd:(hh + 1) * hd, :], p.astype(BF16), preferred_element_type=F32)

    own_rows = pl.ds(pl.multiple_of((i % KV_GROUP) * MOBA_BLOCK, MOBA_BLOCK), MOBA_BLOCK)
    own_cols = pl.ds(pl.multiple_of(i * MOBA_BLOCK, MOBA_BLOCK), MOBA_BLOCK)
    s_own = jnp.dot(k_ref[i // KV_GROUP, own_rows, :], q2, preferred_element_type=F32)
    v_own = vT_ref[:, own_cols]
    carry = []
    for hh in range(2):
        st = jnp.where(causal, s_own[:, hh * MOBA_BLOCK:(hh + 1) * MOBA_BLOCK], NEG_INF)
        m = jnp.max(st, axis=0, keepdims=True)
        p = jnp.exp2(st - m)
        carry += [m, jnp.sum(p, axis=0, keepdims=True), pv(v_own, hh, p)]

    ng = nb // KV_GROUP

    def score_into(s_ref, jg):
        s_ref[...] = jnp.dot(k_ref[jg], q2, preferred_element_type=F32)

    def fold(s_ref, jg, carry):
        v_g = vT_ref[:, pl.ds(pl.multiple_of(jg * KV_SPAN, KV_SPAN), KV_SPAN)]
        out = []
        for hh in range(2):
            m, l, acc = carry[3 * hh:3 * hh + 3]
            sh = lambda g: s_ref[g * MOBA_BLOCK:(g + 1) * MOBA_BLOCK, hh * MOBA_BLOCK:(hh + 1) * MOBA_BLOCK]
            bias = [bias_ref[hh, pl.ds(jg * KV_GROUP + g, 1), :] for g in range(KV_GROUP)]
            m_new = m
            for g in range(KV_GROUP):
                m_new = jnp.maximum(m_new, jnp.max(sh(g), axis=0, keepdims=True) + bias[g])
            p = jnp.concatenate([jnp.exp2(sh(g) - (m_new - bias[g])) for g in range(KV_GROUP)], axis=0)
            alpha = jnp.exp2(m - m_new)
            out += [m_new, alpha * l + jnp.sum(p, axis=0, keepdims=True), alpha * acc + pv(v_g, hh, p)]
        return tuple(out)

    def body(t, carry):
        score_into(sb_ref, 2 * t + 1)
        carry = fold(sa_ref, 2 * t, carry)
        score_into(sa_ref, jnp.minimum(2 * t + 2, ng - 1))
        return fold(sb_ref, 2 * t + 1, carry)

    n_groups = (i + KV_GROUP - 1) // KV_GROUP
    score_into(sa_ref, 0)
    carry = lax.fori_loop(0, (n_groups + 1) // 2, body, tuple(carry))
    oT = jnp.concatenate([carry[2] / carry[1], carry[5] / carry[4]], axis=0)
    o_ref[...] = oT.T.astype(BF16)


def _moba(qT, k, vT, km):
    s = k.shape[0]
    nb = s // MOBA_BLOCK
    assert nb % (2 * KV_GROUP) == 0
    ng = nb // KV_GROUP
    pw = 2 * ATTN_HEAD_DIM
    return pl.pallas_call(
        _moba_kernel, grid=(ATTN_HEADS // 2, nb),
        in_specs=[
            pl.BlockSpec((1, pw, MOBA_BLOCK), lambda p, i: (i, p, 0)),
            pl.BlockSpec((ng, KV_SPAN, pw), lambda p, i: (0, 0, p)),
            pl.BlockSpec((pw, s), lambda p, i: (p, 0)),
            pl.BlockSpec((nb, pw), lambda p, i: (0, p)),
        ],
        out_specs=pl.BlockSpec((MOBA_BLOCK, pw), lambda p, i: (i, p)),
        out_shape=jax.ShapeDtypeStruct((s, ATTN_WIDTH), BF16),
        scratch_shapes=[pltpu.VMEM((2, nb, MOBA_BLOCK), F32),
                        pltpu.VMEM((KV_SPAN, 2 * MOBA_BLOCK), F32), pltpu.VMEM((KV_SPAN, 2 * MOBA_BLOCK), F32)],
        compiler_params=_cparams("arbitrary", "arbitrary"), name="moba",
    )(qT, k.reshape(ng, KV_SPAN, ATTN_WIDTH), vT, km)


def _log_sigmoid(x):
    return -(jnp.maximum(-x, 0.0) + jnp.log1p(jnp.exp(-jnp.abs(x))))


def _mlstm_kernel(qk_ref, v_ref, o_ref, ifc_ref, ifT_ref, cw_ref, cb_ref, ifbr_ref, ifbc_ref, ng_ref,
                  y_ref, ext_ref, c_ref, m_ref):
    L, H, dh = MLSTM_CHUNK, MLSTM_HEADS, MLSTM_HEAD_DIM
    halo = SUBLANES

    @pl.when(pl.program_id(0) == 0)
    def _():
        ext_ref[0:halo, :] = jnp.zeros((halo, 2 * MLSTM_WIDTH), F32)
        c_ref[...] = jnp.zeros(c_ref.shape, F32)
        m_ref[...] = jnp.zeros(m_ref.shape, F32)

    ext_ref[halo:halo + L, :] = qk_ref[...].astype(F32)
    y = cb_ref[...]
    for kk in range(CONV_K):
        off = halo - (CONV_K - 1) + kk
        y = y + ext_ref[off:off + L, :] * cw_ref[kk:kk + 1, :]
    ext_ref[0:halo, :] = ext_ref[L:L + halo, :]
    qk = y * _sigmoid(y)

    r_i = lax.broadcasted_iota(I32, (L, L), 0)
    c_i = lax.broadcasted_iota(I32, (L, L), 1)
    tril = (c_i <= r_i)
    tril_f = tril.astype(F32)
    triu_f = (r_i <= c_i).astype(F32)
    hp = lax.Precision.HIGHEST

    gc = ifc_ref[...] + ifbr_ref[...]
    gr = ifT_ref[...] + ifbc_ref[...]
    bcum_c = jnp.dot(tril_f, _log_sigmoid(gc), precision=hp, preferred_element_type=F32)
    bcum_r = jnp.dot(_log_sigmoid(gr), triu_f, precision=hp, preferred_element_type=F32)

    ones_col = (lax.broadcasted_iota(I32, (L, dh), 1) == 0).astype(BF16)
    m_all = m_ref[...]

    for h in range(H):
        q = qk[:, h * dh:(h + 1) * dh].astype(BF16)
        kf = qk[:, MLSTM_WIDTH + h * dh:MLSTM_WIDTH + (h + 1) * dh] * (dh ** -0.5)
        v_aug = jnp.concatenate([v_ref[:, h * dh:(h + 1) * dh], ones_col], axis=1)

        bc = jnp.broadcast_to(bcum_c[:, H + h:H + h + 1], (L, L))
        li_c = jnp.broadcast_to(gc[:, h:h + 1], (L, L))
        br = bcum_r[H + h:H + h + 1, :]
        li_r = gr[h:h + 1, :]
        m_prev = m_all[h:h + 1, :]

        dmat = jnp.where(tril, bc - br + li_r, NEG_INF)
        inter = bc + m_prev
        m_t = jnp.maximum(inter, jnp.max(dmat, axis=-1, keepdims=True))
        w_ts = jnp.exp(dmat - m_t)
        sc_inter = jnp.exp(inter - m_t)

        sqk = lax.dot_general(q, kf.astype(BF16), NT_DIMS, preferred_element_type=F32) * w_ts
        intra = jnp.dot(sqk.astype(BF16), v_aug, preferred_element_type=F32)
        c_old = c_ref[h]
        cq = jnp.dot(q, c_old.astype(BF16), preferred_element_type=F32)
        num = intra[:, :dh] + sc_inter * cq[:, :dh]
        den = intra[:, dh:dh + 1] + sc_inter[:, 0:1] * cq[:, dh:dh + 1]
        h_t = num / jnp.maximum(jnp.abs(den), jnp.exp(-m_t[:, 0:1]))

        b_last = bc[L - 1:L, :]
        lw = b_last - bc + li_c
        m_new = jnp.maximum(b_last + m_prev, jnp.max(lw, axis=0, keepdims=True))
        w_s = jnp.exp(lw - m_new)
        decay = jnp.exp(b_last + m_prev - m_new)
        kw = (kf * w_s).astype(BF16)
        upd = lax.dot_general(kw, v_aug, TN_DIMS, preferred_element_type=F32)
        c_ref[h] = jnp.concatenate([decay, decay], axis=1) * c_old + upd
        m_ref[h:h + 1, :] = m_new

        hc = _sigmoid(o_ref[:, h * dh:(h + 1) * dh].astype(F32)) * h_t
        y_ref[:, h * dh:(h + 1) * dh] = _rms(hc, ng_ref[:, h * dh:(h + 1) * dh]).astype(BF16)


def _mlstm(mqk, mv, mo, ifc, ifT, conv_w, conv_b, ifb_row, ifb_col, norm_g):
    s = mqk.shape[0]
    L = MLSTM_CHUNK
    row = lambda w: pl.BlockSpec((L, w), lambda i: (i, 0))
    full = lambda a: pl.BlockSpec(a.shape, lambda i: (0,) * a.ndim)
    return pl.pallas_call(
        _mlstm_kernel, grid=(s // L,),
        in_specs=[row(2 * MLSTM_WIDTH), row(MLSTM_WIDTH), row(MLSTM_WIDTH), row(LANES),
                  pl.BlockSpec((SUBLANES, L), lambda i: (0, i)),
                  full(conv_w), full(conv_b), full(ifb_row), full(ifb_col), full(norm_g)],
        out_specs=row(MLSTM_WIDTH),
        out_shape=jax.ShapeDtypeStruct((s, MLSTM_WIDTH), BF16),
        scratch_shapes=[pltpu.VMEM((L + SUBLANES, 2 * MLSTM_WIDTH), F32),
                        pltpu.VMEM((MLSTM_HEADS, MLSTM_HEAD_DIM, 2 * MLSTM_HEAD_DIM), F32),
                        pltpu.VMEM((SUBLANES, L), F32)],
        compiler_params=_cparams("arbitrary"), name="mlstm",
    )(mqk, mv, mo, ifc, ifT, conv_w, conv_b, ifb_row, ifb_col, norm_g)


_R_ROWS = 40


def _merge_kernel(ya_ref, ym_ref, gates_ref, x_ref, wa_ref, wm_ref, wo_ref, bg_ref, fg_ref,
                  wrh_ref, wrl_ref, rb_ref,
                  h1_ref, xn_ref, eid_ref, rank_ref, wcol_ref, cnt_ref, carry_ref):
    tm = TM_MERGE

    @pl.when(pl.program_id(0) == 0)
    def _():
        carry_ref[...] = jnp.zeros(carry_ref.shape, F32)

    gates = _sigmoid(gates_ref[...].astype(F32) + bg_ref[...])
    ua = jnp.dot(ya_ref[...], wa_ref[...], preferred_element_type=F32)
    um = jnp.dot(ym_ref[...], wm_ref[...], preferred_element_type=F32)
    merged = gates[:, :D_MODEL] * ua + gates[:, D_MODEL:] * um
    h1 = x_ref[...] + jnp.dot(merged.astype(BF16), wo_ref[...], preferred_element_type=F32)
    h1_ref[...] = h1
    xn = _rms(h1, fg_ref[...])
    xn_ref[...] = xn

    x_hi = xn.astype(BF16)
    x_lo = (xn - x_hi.astype(F32)).astype(BF16)
    wrh, wrl = wrh_ref[...], wrl_ref[...]
    dot_nt = lambda a, b: lax.dot_general(a, b, NT_DIMS, preferred_element_type=F32)
    lg = dot_nt(wrh, x_hi) + dot_nt(wrh, x_lo) + dot_nt(wrl, x_hi) + rb_ref[:, 0:1]

    G, E = N_GROUPS, EXPERTS_PER_GROUP
    gl = lg[0:G]
    gidx = lax.broadcasted_iota(I32, (G, tm), 0)
    gmax = jnp.max(gl, axis=0, keepdims=True)
    grp = jnp.min(jnp.where(gl == gmax, gidx, G), axis=0, keepdims=True)
    p_grp = 1.0 / jnp.sum(jnp.exp(gl - gmax), axis=0, keepdims=True)
    el = jnp.zeros((E, tm), F32)
    for g in range(G):
        el = jnp.where(grp == g, lg[SUBLANES + g * E:SUBLANES + (g + 1) * E], el)
    ex = jnp.exp(el - jnp.max(el, axis=0, keepdims=True))
    ep = ex / jnp.sum(ex, axis=0, keepdims=True)
    eidx = lax.broadcasted_iota(I32, (E, tm), 0)
    w1 = jnp.max(ep, axis=0, keepdims=True)
    i1 = jnp.min(jnp.where(ep == w1, eidx, E), axis=0, keepdims=True)
    ep2 = jnp.where(eidx == i1, -1.0, ep)
    w2 = jnp.max(ep2, axis=0, keepdims=True)
    i2 = jnp.min(jnp.where(ep2 == w2, eidx, E), axis=0, keepdims=True)
    wsum = w1 + w2
    wt1 = w1 / wsum * p_grp
    wt2 = w2 / wsum * p_grp
    e1 = grp * E + i1
    e2 = grp * E + i2

    xidx = lax.broadcasted_iota(I32, (N_EXPERTS, tm), 0)
    oh1 = xidx == e1
    oh2 = xidx == e2
    member = jnp.where(oh1 | oh2, 1.0, 0.0)
    t_r = lax.broadcasted_iota(I32, (tm, tm), 0)
    t_c = lax.broadcasted_iota(I32, (tm, tm), 1)
    before = (t_r < t_c).astype(BF16)
    cs = jnp.dot(member.astype(BF16), before, preferred_element_type=F32) + carry_ref[:, 0:1]
    r1 = jnp.sum(jnp.where(oh1, cs, 0.0), axis=0, keepdims=True)
    r2 = jnp.sum(jnp.where(oh2, cs, 0.0), axis=0, keepdims=True)
    carry = carry_ref[...] + jnp.sum(member, axis=1, keepdims=True)
    carry_ref[...] = carry
    cnt_ref[...] = carry

    ridx = lax.broadcasted_iota(I32, (SUBLANES, tm), 0)
    eid_ref[...] = jnp.where(ridx == 0, e1, jnp.where(ridx == 1, e2, 0))
    rank_ref[...] = jnp.where(ridx == 0, r1, jnp.where(ridx == 1, r2, 0.0)).astype(I32)
    widx = lax.broadcasted_iota(I32, (LANES, tm), 0)
    wpad = jnp.where(widx == 0, wt1, jnp.where(widx == 1, wt2, 0.0))
    wcol_ref[...] = wpad.T


def _merge(ya, ym, gates, x, wa, wm, wo, bg, fg, wrh, wrl, rb):
    s = x.shape[0]
    tm = TM_MERGE
    row = lambda w: pl.BlockSpec((tm, w), lambda i: (i, 0))
    col = lambda r: pl.BlockSpec((r, tm), lambda i: (0, i))
    full = lambda a: pl.BlockSpec(a.shape, lambda i: (0,) * a.ndim)
    out_shape = (
        jax.ShapeDtypeStruct((s, D_MODEL), F32),
        jax.ShapeDtypeStruct((s, D_MODEL), F32),
        jax.ShapeDtypeStruct((SUBLANES, s), I32),
        jax.ShapeDtypeStruct((SUBLANES, s), I32),
        jax.ShapeDtypeStruct((s, LANES), F32),
        jax.ShapeDtypeStruct((N_EXPERTS, LANES), F32),
    )
    out_specs = (row(D_MODEL), row(D_MODEL), col(SUBLANES), col(SUBLANES), row(LANES),
                 pl.BlockSpec((N_EXPERTS, LANES), lambda i: (0, 0)))
    return pl.pallas_call(
        _merge_kernel, grid=(s // tm,),
        in_specs=[row(ATTN_WIDTH), row(MLSTM_WIDTH), row(2 * D_MODEL), row(D_MODEL),
                  full(wa), full(wm), full(wo), full(bg), full(fg), full(wrh), full(wrl), full(rb)],
        out_specs=out_specs, out_shape=out_shape,
        scratch_shapes=[pltpu.VMEM((N_EXPERTS, LANES), F32)],
        compiler_params=_cparams("arbitrary"), name="merge",
    )(ya, ym, gates, x, wa, wm, wo, bg, fg, wrh, wrl, rb)


def _dest_kernel(eid_ref, rank_ref, poff_ref, dest_ref):
    eid = eid_ref[...]
    xidx = lax.broadcasted_iota(I32, (N_EXPERTS, eid.shape[1]), 0)
    poff = poff_ref[:, 0:1]
    dest_ref[...] = jnp.zeros(dest_ref.shape, I32)
    for kk in range(2):
        base = jnp.sum(jnp.where(xidx == eid[kk:kk + 1, :], poff, 0), axis=0, keepdims=True)
        dest_ref[kk:kk + 1, :] = base + rank_ref[kk:kk + 1, :]


def _dest(eid, rank, poff):
    s = eid.shape[1]
    col = pl.BlockSpec((SUBLANES, IDX_TOK), lambda i: (0, i))
    return pl.pallas_call(
        _dest_kernel, grid=(s // IDX_TOK,),
        in_specs=[col, col, pl.BlockSpec(poff.shape, lambda i: (0, 0))],
        out_specs=col, out_shape=jax.ShapeDtypeStruct((SUBLANES, s), I32),
        compiler_params=_cparams("arbitrary"), name="dest",
    )(eid, rank, poff)


def _load_indices(dest_hbm, idx_ref, sem):
    i = pl.program_id(0)
    per = IDX_TOK // TM_TOK

    @pl.when(i % per == 0)
    def _():
        cols = pl.ds(pl.multiple_of((i // per) * IDX_TOK, IDX_TOK), IDX_TOK)
        cp = pltpu.make_async_copy(dest_hbm.at[:, cols], idx_ref, sem)
        cp.start()
        cp.wait()

    return (i % per) * TM_TOK


def _dispatch_kernel(x_ref, dest_hbm, zeros_hbm, xs_hbm, idx_ref, isem, sem):
    del zeros_hbm
    t0 = _load_indices(dest_hbm, idx_ref, isem)

    def issue(r, c):
        for kk in range(2):
            slot = idx_ref[kk, t0 + r]
            pltpu.make_async_copy(x_ref.at[pl.ds(r, 1)], xs_hbm.at[pl.ds(slot, 1)], sem).start()
        return c

    lax.fori_loop(0, TM_TOK, issue, 0, unroll=ISSUE_UNROLL)
    for _ in range(2):
        pltpu.make_async_copy(x_ref, xs_hbm.at[pl.ds(0, TM_TOK)], sem).wait()


def _dispatch(xn, dest, n_pad):
    s = xn.shape[0]
    zeros = jnp.zeros((n_pad, D_MODEL), F32)
    any_spec = pl.BlockSpec(memory_space=pl.ANY)
    return pl.pallas_call(
        _dispatch_kernel, grid=(s // TM_TOK,),
        in_specs=[pl.BlockSpec((TM_TOK, D_MODEL), lambda i: (i, 0)), any_spec, any_spec],
        out_specs=any_spec,
        out_shape=jax.ShapeDtypeStruct((n_pad, D_MODEL), F32),
        scratch_shapes=[pltpu.SMEM((SUBLANES, IDX_TOK), I32),
                        pltpu.SemaphoreType.DMA, pltpu.SemaphoreType.DMA],
        input_output_aliases={2: 0},
        compiler_params=_cparams("arbitrary"), name="dispatch",
    )(xn, dest, zeros)


def _experts_kernel(be_ref, xs_ref, wg_ref, wu_ref, wd_ref, y_ref):
    del be_ref
    xb = xs_ref[...].astype(BF16)
    g = jnp.dot(xb, wg_ref[0].astype(BF16), preferred_element_type=F32)
    u = jnp.dot(xb, wu_ref[0].astype(BF16), preferred_element_type=F32)
    hid = (g * _sigmoid(g) * u).astype(BF16)
    y_ref[...] = jnp.dot(hid, wd_ref[0].astype(BF16), preferred_element_type=F32)


def _experts(blk_e, xs, wg, wu, wd):
    n_pad = xs.shape[0]
    grid_spec = pltpu.PrefetchScalarGridSpec(
        num_scalar_prefetch=1, grid=(n_pad // ROW_BLOCK,),
        in_specs=[pl.BlockSpec((ROW_BLOCK, D_MODEL), lambda b, be: (b, 0)),
                  pl.BlockSpec((1, D_MODEL, EXPERT_FF), lambda b, be: (be[b], 0, 0)),
                  pl.BlockSpec((1, D_MODEL, EXPERT_FF), lambda b, be: (be[b], 0, 0)),
                  pl.BlockSpec((1, EXPERT_FF, D_MODEL), lambda b, be: (be[b], 0, 0))],
        out_specs=pl.BlockSpec((ROW_BLOCK, D_MODEL), lambda b, be: (b, 0)),
    )
    return pl.pallas_call(
        _experts_kernel, grid_spec=grid_spec,
        out_shape=jax.ShapeDtypeStruct((n_pad, D_MODEL), F32),
        compiler_params=_cparams("arbitrary"), name="experts",
    )(blk_e, xs, wg, wu, wd)


def _final_kernel(h1_ref, p_ref, wcol_ref, dest_hbm, yb_hbm, wpg_ref, wpp_ref, pg_ref, fg_ref,
                  out_ref, idx_ref, g_ref, isem, sems):
    i = pl.program_id(0)
    per = IDX_TOK // TM_TOK

    def fetch(tile):
        @pl.when(tile % per == 0)
        def _():
            cols = pl.ds(pl.multiple_of((tile // per) * IDX_TOK, IDX_TOK), IDX_TOK)
            cp = pltpu.make_async_copy(dest_hbm.at[:, cols], idx_ref, isem)
            cp.start()
            cp.wait()

        t0 = (tile % per) * TM_TOK
        slot = tile % 2

        def issue(r, c):
            for kk in range(2):
                row = idx_ref[kk, t0 + r]
                pltpu.make_async_copy(yb_hbm.at[pl.ds(row, 1)], g_ref.at[slot, kk, pl.ds(r, 1)],
                                      sems.at[slot]).start()
            return c

        lax.fori_loop(0, TM_TOK, issue, 0, unroll=ISSUE_UNROLL)

    @pl.when(i == 0)
    def _():
        fetch(i)

    @pl.when(i + 1 < pl.num_programs(0))
    def _():
        fetch(i + 1)

    slot = i % 2
    for kk in range(2):
        pltpu.make_async_copy(yb_hbm.at[pl.ds(0, TM_TOK)], g_ref.at[slot, kk], sems.at[slot]).wait()

    w = wcol_ref[...]
    h2 = h1_ref[...] + (g_ref[slot, 0] * w[:, 0:1] + g_ref[slot, 1] * w[:, 1:2])
    z = _rms(h2, pg_ref[...]).astype(BF16)
    gate = _sigmoid(jnp.dot(z, wpg_ref[...], preferred_element_type=F32))
    pp = jnp.dot(p_ref[...].astype(BF16), wpp_ref[...], preferred_element_type=F32)
    out_ref[...] = _rms(h2 + pp * gate, fg_ref[...])


def _final(h1, p, wcol, dest, yb, wpg, wpp, pg, fg):
    s = h1.shape[0]
    row = lambda w: pl.BlockSpec((TM_TOK, w), lambda i: (i, 0))
    full = lambda a: pl.BlockSpec(a.shape, lambda i: (0,) * a.ndim)
    any_spec = pl.BlockSpec(memory_space=pl.ANY)
    return pl.pallas_call(
        _final_kernel, grid=(s // TM_TOK,),
        in_specs=[row(D_MODEL), row(PLE_DIM), row(LANES), any_spec, any_spec,
                  full(wpg), full(wpp), full(pg), full(fg)],
        out_specs=row(D_MODEL),
        out_shape=jax.ShapeDtypeStruct((s, D_MODEL), F32),
        scratch_shapes=[pltpu.SMEM((SUBLANES, IDX_TOK), I32),
                        pltpu.VMEM((2, 2, TM_TOK, D_MODEL), F32),
                        pltpu.SemaphoreType.DMA, pltpu.SemaphoreType.DMA((2,))],
        compiler_params=_cparams("arbitrary"), name="final",
    )(h1, p, wcol, dest, yb, wpg, wpp, pg, fg)


def _split_bf16(w):
    hi = w.astype(BF16)
    return hi, (w - hi.astype(F32)).astype(BF16)


def _layer(h, p, mix_norm_g, w_in, b_gate, conv_w, conv_b, mlstm_if_b, mlstm_norm_g,
           w_up_attn, w_up_mlstm, w_out, ffn_norm_g, rg_w, rg_b, re_w, re_b,
           w_gate, w_up, w_down, ple_norm_g, w_ple_gate, w_ple_proj, out_norm_g):
    s = h.shape[0]
    aw, mw, d = ATTN_WIDTH, MLSTM_WIDTH, D_MODEL
    o_mq = 3 * aw
    o_if = o_mq + 4 * mw
    o_g = o_if + 2 * MLSTM_HEADS
    w_if = w_in[:, o_if:o_g]
    wrow = jnp.concatenate(
        [w_in[:, aw:2 * aw], w_in[:, o_mq:o_if], w_in[:, o_g:], w_if,
         jnp.zeros((d, LANES - 2 * MLSTM_HEADS), F32)], axis=1).astype(BF16)
    wt = jnp.concatenate(
        [w_in[:, 0:aw].T, w_in[:, 2 * aw:3 * aw].T, w_if.T,
         jnp.zeros((16 - 2 * MLSTM_HEADS, d), F32)], axis=0).astype(BF16)

    k, kmean, mqk, mv, mo, gates, ifc, qT, vT, ifT = _proj(h, mix_norm_g[None, :], wrow, wt)
    nb = s // MOBA_BLOCK
    ya = _moba(qT, k, vT, kmean.reshape(nb, aw))

    ifb_row = jnp.concatenate([mlstm_if_b, jnp.zeros((LANES - 2 * MLSTM_HEADS,), F32)])[None, :]
    ifb_col = jnp.broadcast_to(mlstm_if_b[:, None], (2 * MLSTM_HEADS, MLSTM_CHUNK))
    ym = _mlstm(mqk, mv, mo, ifc, ifT, conv_w, conv_b[None, :], ifb_row, ifb_col, mlstm_norm_g[None, :])

    wr = jnp.concatenate([rg_w.T, jnp.zeros((SUBLANES - N_GROUPS, d), F32), re_w.T], axis=0)
    wrh, wrl = _split_bf16(wr)
    rb = jnp.concatenate([rg_b, jnp.zeros((SUBLANES - N_GROUPS,), F32), re_b])
    rb = jnp.broadcast_to(rb[:, None], (_R_ROWS, LANES))
    h1, xn, eid, rank, wcol, cnt = _merge(
        ya, ym, gates, h, w_up_attn.astype(BF16), w_up_mlstm.astype(BF16), w_out.astype(BF16),
        b_gate[None, :], ffn_norm_g[None, :], wrh, wrl, rb)

    n_assign = 2 * s
    n_pad = (n_assign + N_EXPERTS * (ROW_BLOCK - 1) + ROW_BLOCK - 1) // ROW_BLOCK * ROW_BLOCK
    counts = cnt[:, 0].astype(I32)
    pcounts = (counts + ROW_BLOCK - 1) // ROW_BLOCK * ROW_BLOCK
    pends = jnp.cumsum(pcounts)
    poffs = pends - pcounts
    blk_start = jnp.arange(n_pad // ROW_BLOCK, dtype=I32) * ROW_BLOCK
    blk_e = jnp.minimum(jnp.sum((pends[None, :] <= blk_start[:, None]).astype(I32), axis=1), N_EXPERTS - 1)
    dest = _dest(eid, rank, jnp.broadcast_to(poffs[:, None], (N_EXPERTS, LANES)))

    xs = _dispatch(xn, dest, n_pad)
    yb = _experts(blk_e, xs, w_gate, w_up, w_down)
    return _final(h1, p, wcol, dest, yb, w_ple_gate.astype(BF16), w_ple_proj.astype(BF16),
                  ple_norm_g[None, :], out_norm_g[None, :])


def kernel(x, p, mix_norm_g, w_in, b_gate, conv_w, conv_b, mlstm_if_b, mlstm_norm_g, w_up_attn, w_up_mlstm, w_out, ffn_norm_g, router_group_w, router_group_b, router_expert_w, router_expert_b, expert_w_gate, expert_w_up, expert_w_down, ple_norm_g, w_ple_gate, w_ple_proj, final_norm_g):
    assert w_in.shape[0] == 1 and x.shape[0] == 1, "one layer, one sequence"
    assert x.shape[1] % IDX_TOK == 0
    out = _layer(x[0], p[0, 0], mix_norm_g[0], w_in[0], b_gate[0], conv_w[0], conv_b[0], mlstm_if_b[0],
                 mlstm_norm_g[0], w_up_attn[0], w_up_mlstm[0], w_out[0], ffn_norm_g[0],
                 router_group_w[0], router_group_b[0], router_expert_w[0], router_expert_b[0],
                 expert_w_gate[0], expert_w_up[0], expert_w_down[0], ple_norm_g[0], w_ple_gate[0],
                 w_ple_proj[0], final_norm_g)
    return out[None]
```

```python
import functools

import jax
import jax.numpy as jnp
from jax import lax
from jax.experimental import pallas as pl
from jax.experimental.pallas import tpu as pltpu

F32 = jnp.float32
BF16 = jnp.bfloat16
I32 = jnp.int32

D_MODEL = 1024
ATTN_HEADS = 8
ATTN_HEAD_DIM = 64
ATTN_WIDTH = ATTN_HEADS * ATTN_HEAD_DIM
MOBA_BLOCK = 256
MOBA_TOPK = 3
KV_GROUP = 2
KV_SPAN = KV_GROUP * MOBA_BLOCK
LOG2E = 1.4426950408889634
BOUND_MARGIN = 1.02
EXP_HEADROOM = 64.0
MAX_SHIFT_GAP = 150.0
FAST_KEYS = 128
Q_BLOCKS = 2
TQ = Q_BLOCKS * MOBA_BLOCK
MLSTM_HEADS = 4
MLSTM_HEAD_DIM = 128
MLSTM_WIDTH = MLSTM_HEADS * MLSTM_HEAD_DIM
MLSTM_CHUNK = 128
CONV_K = 4
N_GROUPS = 4
EXPERTS_PER_GROUP = 8
N_EXPERTS = N_GROUPS * EXPERTS_PER_GROUP
EXPERT_FF = 512
ROW_BLOCK = 256
PLE_DIM = 256
EPS = 1e-6

LANES = 128
SUBLANES = 8
VMEM_LIMIT = 56 * 1024 * 1024

TM_PROJ = 512
TM_MERGE = 512
TM_TOK = 256
IDX_TOK = 1024
ISSUE_UNROLL = 8

NT_DIMS = (((1,), (1,)), ((), ()))
TN_DIMS = (((0,), (0,)), ((), ()))
NEG_INF = float("-inf")


def _cparams(*sem):
    return pltpu.CompilerParams(dimension_semantics=sem, vmem_limit_bytes=VMEM_LIMIT)


def _rms(x, g):
    return x * lax.rsqrt(jnp.mean(x * x, axis=-1, keepdims=True) + EPS) * g


def _sigmoid(x):
    return 1.0 / (1.0 + jnp.exp(-x))


_C_K, _C_QK, _C_V, _C_O, _C_G, _C_IF, _C_END = 0, 512, 1536, 2048, 2560, 4608, 4736


def _proj_kernel(x_ref, g_ref, wrow_ref, wt_ref,
                 k_ref, kstat_ref, mqk_ref, mv_ref, mo_ref, gates_ref, ifc_ref,
                 qT_ref, vT_ref, ifT_ref):
    xb = _rms(x_ref[...], g_ref[...]).astype(BF16)

    def rowdot(a, b):
        return jnp.dot(xb, wrow_ref[:, a:b], preferred_element_type=F32)

    k = rowdot(_C_K, _C_QK)
    kb = k.astype(BF16)
    k_ref[...] = kb
    ksq = jnp.square(kb.astype(F32))
    for b in range(TM_PROJ // MOBA_BLOCK):
        rows = slice(b * MOBA_BLOCK, (b + 1) * MOBA_BLOCK)
        kstat_ref[b, 0:1, :] = jnp.mean(k[rows], axis=0, keepdims=True)
        kstat_ref[b, 1:2, :] = jnp.max(ksq[rows], axis=0, keepdims=True)
    mqk_ref[...] = rowdot(_C_QK, _C_V).astype(BF16)
    mv_ref[...] = rowdot(_C_V, _C_O).astype(BF16)
    mo_ref[...] = rowdot(_C_O, _C_G).astype(BF16)
    gates_ref[...] = rowdot(_C_G, _C_IF).astype(BF16)
    ifc_ref[...] = rowdot(_C_IF, _C_END)

    def colT(a, b):
        return lax.dot_general(wt_ref[a:b, :], xb, NT_DIMS, preferred_element_type=F32)

    qT = colT(0, ATTN_WIDTH) * (ATTN_HEAD_DIM ** -0.5 * LOG2E)
    for b in range(TM_PROJ // MOBA_BLOCK):
        qT_ref[b] = qT[:, b * MOBA_BLOCK:(b + 1) * MOBA_BLOCK].astype(BF16)
    vT_ref[...] = colT(ATTN_WIDTH, 2 * ATTN_WIDTH).astype(BF16)
    ifT_ref[...] = colT(2 * ATTN_WIDTH, 2 * ATTN_WIDTH + 16)[0:SUBLANES]


def _proj(x, g, wrow, wt):
    s = x.shape[0]
    nb = s // MOBA_BLOCK
    bpt = TM_PROJ // MOBA_BLOCK
    row = lambda w: pl.BlockSpec((TM_PROJ, w), lambda i: (i, 0))
    full = lambda a: pl.BlockSpec(a.shape, lambda i: (0,) * a.ndim)
    out_shape = (
        jax.ShapeDtypeStruct((s, ATTN_WIDTH), BF16),
        jax.ShapeDtypeStruct((nb, 2, ATTN_WIDTH), F32),
        jax.ShapeDtypeStruct((s, 2 * MLSTM_WIDTH), BF16),
        jax.ShapeDtypeStruct((s, MLSTM_WIDTH), BF16),
        jax.ShapeDtypeStruct((s, MLSTM_WIDTH), BF16),
        jax.ShapeDtypeStruct((s, 2 * D_MODEL), BF16),
        jax.ShapeDtypeStruct((s, LANES), F32),
        jax.ShapeDtypeStruct((nb, ATTN_WIDTH, MOBA_BLOCK), BF16),
        jax.ShapeDtypeStruct((ATTN_WIDTH, s), BF16),
        jax.ShapeDtypeStruct((SUBLANES, s), F32),
    )
    out_specs = (
        row(ATTN_WIDTH),
        pl.BlockSpec((bpt, 2, ATTN_WIDTH), lambda i: (i, 0, 0)),
        row(2 * MLSTM_WIDTH), row(MLSTM_WIDTH), row(MLSTM_WIDTH), row(2 * D_MODEL), row(LANES),
        pl.BlockSpec((bpt, ATTN_WIDTH, MOBA_BLOCK), lambda i: (i, 0, 0)),
        pl.BlockSpec((ATTN_WIDTH, TM_PROJ), lambda i: (0, i)),
        pl.BlockSpec((SUBLANES, TM_PROJ), lambda i: (0, i)),
    )
    return pl.pallas_call(
        _proj_kernel, grid=(s // TM_PROJ,),
        in_specs=[row(D_MODEL), full(g), full(wrow), full(wt)],
        out_specs=out_specs, out_shape=out_shape,
        compiler_params=_cparams("arbitrary"), name="proj",
    )(x, g, wrow, wt)


def _moba_kernel(qT_ref, k_ref, vT_ref, km_ref, kx_ref, o_ref, bias_ref, sa_ref, sb_ref, oT_ref):
    i = pl.program_id(1)
    nb = km_ref.shape[0]
    hd = ATTN_HEAD_DIM
    qT = qT_ref[0]
    row = lax.broadcasted_iota(I32, qT.shape, 0)
    km = km_ref[...]
    km_hi = km.astype(BF16)
    km_lo = (km - km_hi.astype(F32)).astype(BF16)
    blk = lax.broadcasted_iota(I32, (nb, MOBA_BLOCK), 0)

    qms = []
    for hh in range(2):
        qm = jnp.where((row >= hh * hd) & (row < (hh + 1) * hd), qT, jnp.zeros_like(qT))
        qms.append(qm)
        gate = (jnp.dot(km_hi, qm, preferred_element_type=F32)
                + jnp.dot(km_lo, qm, preferred_element_type=F32))
        g = jnp.where(blk < i, gate, NEG_INF)
        sel = jnp.zeros(g.shape, F32)
        for _ in range(MOBA_TOPK):
            mx = jnp.max(g, axis=0, keepdims=True)
            first = jnp.min(jnp.where(g == mx, blk, nb), axis=0, keepdims=True)
            pick = (blk == first) & (mx > NEG_INF)
            sel = jnp.where(pick, 1.0, sel)
            g = jnp.where(pick, NEG_INF, g)
        bias_ref[hh] = jnp.where(sel > 0.0, 0.0, NEG_INF)
    q2 = jnp.concatenate(qms, axis=1)

    kpos = lax.broadcasted_iota(I32, (MOBA_BLOCK, MOBA_BLOCK), 0)
    qpos = lax.broadcasted_iota(I32, (MOBA_BLOCK, MOBA_BLOCK), 1)
    causal = kpos <= qpos

    def pv(v, hh, p):
        return jnp.dot(v[hh * hd:(hh + 1) * hd, :], p.astype(BF16), preferred_element_type=F32)

    own_rows = pl.ds(pl.multiple_of((i % KV_GROUP) * MOBA_BLOCK, MOBA_BLOCK), MOBA_BLOCK)
    own_cols = pl.ds(pl.multiple_of(i * MOBA_BLOCK, MOBA_BLOCK), MOBA_BLOCK)
    s_own = jnp.dot(k_ref[i // KV_GROUP, own_rows, :], q2, preferred_element_type=F32)
    v_own = vT_ref[:, own_cols]
    carry = []
    for hh in range(2):
        st = jnp.where(causal, s_own[:, hh * MOBA_BLOCK:(hh + 1) * MOBA_BLOCK], NEG_INF)
        m = jnp.max(st, axis=0, keepdims=True)
        p = jnp.exp2(st - m)
        carry += [m, jnp.sum(p, axis=0, keepdims=True), pv(v_own, hh, p)]

    ng = nb // KV_GROUP

    def score_into(s_ref, jg):
        s_ref[...] = jnp.dot(k_ref[jg], q2, preferred_element_type=F32)

    def fold(s_ref, jg, carry):
        v_g = vT_ref[:, pl.ds(pl.multiple_of(jg * KV_SPAN, KV_SPAN), KV_SPAN)]
        out = []
        for hh in range(2):
            m, l, acc = carry[3 * hh:3 * hh + 3]
            sh = lambda g: s_ref[g * MOBA_BLOCK:(g + 1) * MOBA_BLOCK, hh * MOBA_BLOCK:(hh + 1) * MOBA_BLOCK]
            bias = [bias_ref[hh, pl.ds(jg * KV_GROUP + g, 1), :] for g in range(KV_GROUP)]
            m_new = m
            for g in range(KV_GROUP):
                m_new = jnp.maximum(m_new, jnp.max(sh(g), axis=0, keepdims=True) + bias[g])
            p = jnp.concatenate([jnp.exp2(sh(g) - (m_new - bias[g])) for g in range(KV_GROUP)], axis=0)
            alpha = jnp.exp2(m - m_new)
            out += [m_new, alpha * l + jnp.sum(p, axis=0, keepdims=True), alpha * acc + pv(v_g, hh, p)]
        return tuple(out)

    def body(t, carry):
        score_into(sb_ref, 2 * t + 1)
        carry = fold(sa_ref, 2 * t, carry)
        score_into(sa_ref, jnp.minimum(2 * t + 2, ng - 1))
        return fold(sb_ref, 2 * t + 1, carry)

    n_groups = (i + KV_GROUP - 1) // KV_GROUP

    kn2 = jnp.max(kx_ref[...], axis=0, keepdims=True)
    shifts, gaps = [], []
    for hh in range(2):
        qf = qms[hh].astype(F32)
        qn2 = jnp.sum(qf * qf, axis=0, keepdims=True)
        kn2_h = jnp.sum(kn2[:, hh * hd:(hh + 1) * hd], axis=1, keepdims=True)
        bound = jnp.sqrt(qn2 * kn2_h) * BOUND_MARGIN
        shifts.append(jnp.maximum(carry[3 * hh], bound - EXP_HEADROOM))
        gaps.append(bound - carry[3 * hh])
    safe = jnp.max(jnp.maximum(gaps[0], gaps[1])) <= MAX_SHIFT_GAP

    @pl.when(safe)
    def _():
        def fast_group(jg, lc):
            v_g = vT_ref[:, pl.ds(pl.multiple_of(jg * KV_SPAN, KV_SPAN), KV_SPAN)]
            out = list(lc)
            for g in range(KV_GROUP):
                for c in range(MOBA_BLOCK // FAST_KEYS):
                    ks = slice(g * MOBA_BLOCK + c * FAST_KEYS, g * MOBA_BLOCK + (c + 1) * FAST_KEYS)
                    s = jnp.dot(k_ref[jg, ks, :], q2, preferred_element_type=F32)
                    for hh in range(2):
                        bias = bias_ref[hh, pl.ds(jg * KV_GROUP + g, 1), :]
                        p = jnp.exp2(s[:, hh * MOBA_BLOCK:(hh + 1) * MOBA_BLOCK] - (shifts[hh] - bias))
                        out[2 * hh] = out[2 * hh] + jnp.sum(p, axis=0, keepdims=True)
                        out[2 * hh + 1] = out[2 * hh + 1] + pv(v_g[:, ks], hh, p)
            return tuple(out)

        def fast_body(t, lc):
            return fast_group(2 * t + 1, fast_group(2 * t, lc))

        init = []
        for hh in range(2):
            f = jnp.exp2(carry[3 * hh] - shifts[hh])
            init += [carry[3 * hh + 1] * f, carry[3 * hh + 2] * f]
        lc = lax.fori_loop(0, (n_groups + 1) // 2, fast_body, tuple(init))
        oT_ref[...] = jnp.concatenate([lc[1] / lc[0], lc[3] / lc[2]], axis=0)

    @pl.when(jnp.logical_not(safe))
    def _():
        score_into(sa_ref, 0)
        c = lax.fori_loop(0, (n_groups + 1) // 2, body, tuple(carry))
        oT_ref[...] = jnp.concatenate([c[2] / c[1], c[5] / c[4]], axis=0)

    o_ref[...] = oT_ref[...].T.astype(BF16)


def _moba(qT, k, vT, km, kx):
    s = k.shape[0]
    nb = s // MOBA_BLOCK
    assert nb % (2 * KV_GROUP) == 0
    ng = nb // KV_GROUP
    pw = 2 * ATTN_HEAD_DIM
    return pl.pallas_call(
        _moba_kernel, grid=(ATTN_HEADS // 2, nb),
        in_specs=[
            pl.BlockSpec((1, pw, MOBA_BLOCK), lambda p, i: (i, p, 0)),
            pl.BlockSpec((ng, KV_SPAN, pw), lambda p, i: (0, 0, p)),
            pl.BlockSpec((pw, s), lambda p, i: (p, 0)),
            pl.BlockSpec((nb, pw), lambda p, i: (0, p)),
            pl.BlockSpec((nb, pw), lambda p, i: (0, p)),
        ],
        out_specs=pl.BlockSpec((MOBA_BLOCK, pw), lambda p, i: (i, p)),
        out_shape=jax.ShapeDtypeStruct((s, ATTN_WIDTH), BF16),
        scratch_shapes=[pltpu.VMEM((2, nb, MOBA_BLOCK), F32),
                        pltpu.VMEM((KV_SPAN, 2 * MOBA_BLOCK), F32), pltpu.VMEM((KV_SPAN, 2 * MOBA_BLOCK), F32),
                        pltpu.VMEM((pw, MOBA_BLOCK), F32)],
        compiler_params=_cparams("arbitrary", "arbitrary"), name="moba",
    )(qT, k.reshape(ng, KV_SPAN, ATTN_WIDTH), vT, km, kx)


def _moba2_kernel(qT_ref, k_ref, vT_ref, km_ref, o_ref, bias_ref, sa_ref, sb_ref):
    c = pl.program_id(1)
    nb = km_ref.shape[0]
    ng = nb // Q_BLOCKS
    hd, B, QB = ATTN_HEAD_DIM, MOBA_BLOCK, Q_BLOCKS
    qT = jnp.concatenate([qT_ref[b] for b in range(QB)], axis=1)
    row = lax.broadcasted_iota(I32, qT.shape, 0)
    km = km_ref[...]
    km_hi = km.astype(BF16)
    km_lo = (km - km_hi.astype(F32)).astype(BF16)
    blk = lax.broadcasted_iota(I32, (nb, TQ), 0)
    lane = lax.broadcasted_iota(I32, (nb, TQ), 1)
    cur = c * QB
    for b in range(1, QB):
        cur = cur + (lane >= b * B).astype(I32)

    qms = []
    for hh in range(2):
        qm = jnp.where((row >= hh * hd) & (row < (hh + 1) * hd), qT, jnp.zeros_like(qT))
        qms.append(qm)
        gate = (jnp.dot(km_hi, qm, preferred_element_type=F32)
                + jnp.dot(km_lo, qm, preferred_element_type=F32))
        g = jnp.where(blk < cur, gate, NEG_INF)
        sel = jnp.zeros(g.shape, F32)
        for _ in range(MOBA_TOPK):
            mx = jnp.max(g, axis=0, keepdims=True)
            first = jnp.min(jnp.where(g == mx, blk, nb), axis=0, keepdims=True)
            pick = (blk == first) & (mx > NEG_INF)
            sel = jnp.where(pick, 1.0, sel)
            g = jnp.where(pick, NEG_INF, g)
        bias_ref[hh] = jnp.where(sel > 0.0, 0.0, NEG_INF)
    q2 = jnp.concatenate(qms, axis=1)

    def pv(v, hh, p):
        return jnp.dot(v[hh * hd:(hh + 1) * hd, :], p.astype(BF16), preferred_element_type=F32)

    causal = (lax.broadcasted_iota(I32, (B, B), 0) <= lax.broadcasted_iota(I32, (B, B), 1))
    s_own = jnp.dot(k_ref[c], q2, preferred_element_type=F32)
    v_own = vT_ref[:, pl.ds(pl.multiple_of(c * TQ, TQ), TQ)]
    carry = []
    for hh in range(2):
        key_rows = []
        for a in range(QB):
            tiles = []
            for b in range(QB):
                t = s_own[a * B:(a + 1) * B, hh * TQ + b * B:hh * TQ + (b + 1) * B]
                if a > b:
                    t = jnp.full((B, B), NEG_INF, F32)
                elif a == b:
                    t = jnp.where(causal, t, NEG_INF)
                else:
                    t = t + bias_ref[hh, pl.ds(c * QB + a, 1), b * B:(b + 1) * B]
                tiles.append(t)
            key_rows.append(jnp.concatenate(tiles, axis=1))
        st = jnp.concatenate(key_rows, axis=0)
        m = jnp.max(st, axis=0, keepdims=True)
        p = jnp.exp2(st - m)
        carry += [m, jnp.sum(p, axis=0, keepdims=True), pv(v_own, hh, p)]

    def score_into(s_ref, jg):
        s_ref[...] = jnp.dot(k_ref[jg], q2, preferred_element_type=F32)

    def fold(s_ref, jg, carry):
        v_g = vT_ref[:, pl.ds(pl.multiple_of(jg * TQ, TQ), TQ)]
        out = []
        for hh in range(2):
            m, l, acc = carry[3 * hh:3 * hh + 3]
            sh = lambda g: s_ref[g * B:(g + 1) * B, hh * TQ:(hh + 1) * TQ]
            bias = [jnp.where(jg < c, bias_ref[hh, pl.ds(jg * QB + g, 1), :], NEG_INF) for g in range(QB)]
            m_new = m
            for g in range(QB):
                m_new = jnp.maximum(m_new, jnp.max(sh(g), axis=0, keepdims=True) + bias[g])
            p = jnp.concatenate([jnp.exp2(sh(g) - (m_new - bias[g])) for g in range(QB)], axis=0)
            alpha = jnp.exp2(m - m_new)
            out += [m_new, alpha * l + jnp.sum(p, axis=0, keepdims=True), alpha * acc + pv(v_g, hh, p)]
        return tuple(out)

    def body(t, carry):
        score_into(sb_ref, jnp.minimum(2 * t + 1, ng - 1))
        carry = fold(sa_ref, 2 * t, carry)
        score_into(sa_ref, jnp.minimum(2 * t + 2, ng - 1))
        return fold(sb_ref, jnp.minimum(2 * t + 1, ng - 1), carry)

    score_into(sa_ref, 0)
    carry = lax.fori_loop(0, (c + 1) // 2, body, tuple(carry))
    oT = jnp.concatenate([carry[2] / carry[1], carry[5] / carry[4]], axis=0)
    o_ref[...] = oT.T.astype(BF16)


def _moba2(qT, k, vT, km):
    s = k.shape[0]
    nb = s // MOBA_BLOCK
    assert nb % Q_BLOCKS == 0
    ng = nb // Q_BLOCKS
    pw = 2 * ATTN_HEAD_DIM
    return pl.pallas_call(
        _moba2_kernel, grid=(ATTN_HEADS // 2, ng),
        in_specs=[
            pl.BlockSpec((Q_BLOCKS, pw, MOBA_BLOCK), lambda p, c: (c, p, 0)),
            pl.BlockSpec((ng, TQ, pw), lambda p, c: (0, 0, p)),
            pl.BlockSpec((pw, s), lambda p, c: (p, 0)),
            pl.BlockSpec((nb, pw), lambda p, c: (0, p)),
        ],
        out_specs=pl.BlockSpec((TQ, pw), lambda p, c: (c, p)),
        out_shape=jax.ShapeDtypeStruct((s, ATTN_WIDTH), BF16),
        scratch_shapes=[pltpu.VMEM((2, nb, TQ), F32),
                        pltpu.VMEM((TQ, 2 * TQ), F32), pltpu.VMEM((TQ, 2 * TQ), F32)],
        compiler_params=_cparams("arbitrary", "arbitrary"), name="moba",
    )(qT, k.reshape(ng, TQ, ATTN_WIDTH), vT, km)


def _log_sigmoid(x):
    return -(jnp.maximum(-x, 0.0) + jnp.log1p(jnp.exp(-jnp.abs(x))))


def _mlstm_kernel(qk_ref, v_ref, o_ref, ifc_ref, ifT_ref, cw_ref, cb_ref, ifbr_ref, ifbc_ref, ng_ref,
                  y_ref, ext_ref, c_ref, m_ref):
    L, H, dh = MLSTM_CHUNK, MLSTM_HEADS, MLSTM_HEAD_DIM
    halo = SUBLANES

    @pl.when(pl.program_id(0) == 0)
    def _():
        ext_ref[0:halo, :] = jnp.zeros((halo, 2 * MLSTM_WIDTH), F32)
        c_ref[...] = jnp.zeros(c_ref.shape, F32)
        m_ref[...] = jnp.zeros(m_ref.shape, F32)

    ext_ref[halo:halo + L, :] = qk_ref[...].astype(F32)
    y = cb_ref[...]
    for kk in range(CONV_K):
        off = halo - (CONV_K - 1) + kk
        y = y + ext_ref[off:off + L, :] * cw_ref[kk:kk + 1, :]
    ext_ref[0:halo, :] = ext_ref[L:L + halo, :]
    qk = y * _sigmoid(y)

    r_i = lax.broadcasted_iota(I32, (L, L), 0)
    c_i = lax.broadcasted_iota(I32, (L, L), 1)
    tril = (c_i <= r_i)
    tril_f = tril.astype(F32)
    triu_f = (r_i <= c_i).astype(F32)
    hp = lax.Precision.HIGHEST

    gc = ifc_ref[...] + ifbr_ref[...]
    gr = ifT_ref[...] + ifbc_ref[...]
    bcum_c = jnp.dot(tril_f, _log_sigmoid(gc), precision=hp, preferred_element_type=F32)
    bcum_r = jnp.dot(_log_sigmoid(gr), triu_f, precision=hp, preferred_element_type=F32)

    ones_col = (lax.broadcasted_iota(I32, (L, dh), 1) == 0).astype(BF16)
    m_all = m_ref[...]

    for h in range(H):
        q = qk[:, h * dh:(h + 1) * dh].astype(BF16)
        kf = qk[:, MLSTM_WIDTH + h * dh:MLSTM_WIDTH + (h + 1) * dh] * (dh ** -0.5)
        v_aug = jnp.concatenate([v_ref[:, h * dh:(h + 1) * dh], ones_col], axis=1)

        bc = jnp.broadcast_to(bcum_c[:, H + h:H + h + 1], (L, L))
        li_c = jnp.broadcast_to(gc[:, h:h + 1], (L, L))
        br = bcum_r[H + h:H + h + 1, :]
        li_r = gr[h:h + 1, :]
        m_prev = m_all[h:h + 1, :]

        dmat = jnp.where(tril, bc - br + li_r, NEG_INF)
        inter = bc + m_prev
        m_t = jnp.maximum(inter, jnp.max(dmat, axis=-1, keepdims=True))
        w_ts = jnp.exp(dmat - m_t)
        sc_inter = jnp.exp(inter - m_t)

        sqk = lax.dot_general(q, kf.astype(BF16), NT_DIMS, preferred_element_type=F32) * w_ts
        intra = jnp.dot(sqk.astype(BF16), v_aug, preferred_element_type=F32)
        c_old = c_ref[h]
        cq = jnp.dot(q, c_old.astype(BF16), preferred_element_type=F32)
        num = intra[:, :dh] + sc_inter * cq[:, :dh]
        den = intra[:, dh:dh + 1] + sc_inter[:, 0:1] * cq[:, dh:dh + 1]
        h_t = num / jnp.maximum(jnp.abs(den), jnp.exp(-m_t[:, 0:1]))

        b_last = bc[L - 1:L, :]
        lw = b_last - bc + li_c
        m_new = jnp.maximum(b_last + m_prev, jnp.max(lw, axis=0, keepdims=True))
        w_s = jnp.exp(lw - m_new)
        decay = jnp.exp(b_last + m_prev - m_new)
        kw = (kf * w_s).astype(BF16)
        upd = lax.dot_general(kw, v_aug, TN_DIMS, preferred_element_type=F32)
        c_ref[h] = jnp.concatenate([decay, decay], axis=1) * c_old + upd
        m_ref[h:h + 1, :] = m_new

        hc = _sigmoid(o_ref[:, h * dh:(h + 1) * dh].astype(F32)) * h_t
        y_ref[:, h * dh:(h + 1) * dh] = _rms(hc, ng_ref[:, h * dh:(h + 1) * dh]).astype(BF16)


def _mlstm(mqk, mv, mo, ifc, ifT, conv_w, conv_b, ifb_row, ifb_col, norm_g):
    s = mqk.shape[0]
    L = MLSTM_CHUNK
    row = lambda w: pl.BlockSpec((L, w), lambda i: (i, 0))
    full = lambda a: pl.BlockSpec(a.shape, lambda i: (0,) * a.ndim)
    return pl.pallas_call(
        _mlstm_kernel, grid=(s // L,),
        in_specs=[row(2 * MLSTM_WIDTH), row(MLSTM_WIDTH), row(MLSTM_WIDTH), row(LANES),
                  pl.BlockSpec((SUBLANES, L), lambda i: (0, i)),
                  full(conv_w), full(conv_b), full(ifb_row), full(ifb_col), full(norm_g)],
        out_specs=row(MLSTM_WIDTH),
        out_shape=jax.ShapeDtypeStruct((s, MLSTM_WIDTH), BF16),
        scratch_shapes=[pltpu.VMEM((L + SUBLANES, 2 * MLSTM_WIDTH), F32),
                        pltpu.VMEM((MLSTM_HEADS, MLSTM_HEAD_DIM, 2 * MLSTM_HEAD_DIM), F32),
                        pltpu.VMEM((SUBLANES, L), F32)],
        compiler_params=_cparams("arbitrary"), name="mlstm",
    )(mqk, mv, mo, ifc, ifT, conv_w, conv_b, ifb_row, ifb_col, norm_g)


_R_ROWS = 40


def _merge_kernel(ya_ref, ym_ref, gates_ref, x_ref, wa_ref, wm_ref, wo_ref, bg_ref, fg_ref,
                  wrh_ref, wrl_ref, rb_ref,
                  h1_ref, xn_ref, eid_ref, rank_ref, wcol_ref, cnt_ref, carry_ref):
    tm = TM_MERGE

    @pl.when(pl.program_id(0) == 0)
    def _():
        carry_ref[...] = jnp.zeros(carry_ref.shape, F32)

    gates = _sigmoid(gates_ref[...].astype(F32) + bg_ref[...])
    ua = jnp.dot(ya_ref[...], wa_ref[...], preferred_element_type=F32)
    um = jnp.dot(ym_ref[...], wm_ref[...], preferred_element_type=F32)
    merged = gates[:, :D_MODEL] * ua + gates[:, D_MODEL:] * um
    h1 = x_ref[...] + jnp.dot(merged.astype(BF16), wo_ref[...], preferred_element_type=F32)
    h1_ref[...] = h1
    xn = _rms(h1, fg_ref[...])
    xn_ref[...] = xn

    x_hi = xn.astype(BF16)
    x_lo = (xn - x_hi.astype(F32)).astype(BF16)
    wrh, wrl = wrh_ref[...], wrl_ref[...]
    dot_nt = lambda a, b: lax.dot_general(a, b, NT_DIMS, preferred_element_type=F32)
    lg = dot_nt(wrh, x_hi) + dot_nt(wrh, x_lo) + dot_nt(wrl, x_hi) + rb_ref[:, 0:1]

    G, E = N_GROUPS, EXPERTS_PER_GROUP
    gl = lg[0:G]
    gidx = lax.broadcasted_iota(I32, (G, tm), 0)
    gmax = jnp.max(gl, axis=0, keepdims=True)
    grp = jnp.min(jnp.where(gl == gmax, gidx, G), axis=0, keepdims=True)
    p_grp = 1.0 / jnp.sum(jnp.exp(gl - gmax), axis=0, keepdims=True)
    el = jnp.zeros((E, tm), F32)
    for g in range(G):
        el = jnp.where(grp == g, lg[SUBLANES + g * E:SUBLANES + (g + 1) * E], el)
    ex = jnp.exp(el - jnp.max(el, axis=0, keepdims=True))
    ep = ex / jnp.sum(ex, axis=0, keepdims=True)
    eidx = lax.broadcasted_iota(I32, (E, tm), 0)
    w1 = jnp.max(ep, axis=0, keepdims=True)
    i1 = jnp.min(jnp.where(ep == w1, eidx, E), axis=0, keepdims=True)
    ep2 = jnp.where(eidx == i1, -1.0, ep)
    w2 = jnp.max(ep2, axis=0, keepdims=True)
    i2 = jnp.min(jnp.where(ep2 == w2, eidx, E), axis=0, keepdims=True)
    wsum = w1 + w2
    wt1 = w1 / wsum * p_grp
    wt2 = w2 / wsum * p_grp
    e1 = grp * E + i1
    e2 = grp * E + i2

    xidx = lax.broadcasted_iota(I32, (N_EXPERTS, tm), 0)
    oh1 = xidx == e1
    oh2 = xidx == e2
    member = jnp.where(oh1 | oh2, 1.0, 0.0)
    t_r = lax.broadcasted_iota(I32, (tm, tm), 0)
    t_c = lax.broadcasted_iota(I32, (tm, tm), 1)
    before = (t_r < t_c).astype(BF16)
    cs = jnp.dot(member.astype(BF16), before, preferred_element_type=F32) + carry_ref[:, 0:1]
    r1 = jnp.sum(jnp.where(oh1, cs, 0.0), axis=0, keepdims=True)
    r2 = jnp.sum(jnp.where(oh2, cs, 0.0), axis=0, keepdims=True)
    carry = carry_ref[...] + jnp.sum(member, axis=1, keepdims=True)
    carry_ref[...] = carry
    cnt_ref[...] = carry

    ridx = lax.broadcasted_iota(I32, (SUBLANES, tm), 0)
    eid_ref[...] = jnp.where(ridx == 0, e1, jnp.where(ridx == 1, e2, 0))
    rank_ref[...] = jnp.where(ridx == 0, r1, jnp.where(ridx == 1, r2, 0.0)).astype(I32)
    widx = lax.broadcasted_iota(I32, (LANES, tm), 0)
    wpad = jnp.where(widx == 0, wt1, jnp.where(widx == 1, wt2, 0.0))
    wcol_ref[...] = wpad.T


def _merge(ya, ym, gates, x, wa, wm, wo, bg, fg, wrh, wrl, rb):
    s = x.shape[0]
    tm = TM_MERGE
    row = lambda w: pl.BlockSpec((tm, w), lambda i: (i, 0))
    col = lambda r: pl.BlockSpec((r, tm), lambda i: (0, i))
    full = lambda a: pl.BlockSpec(a.shape, lambda i: (0,) * a.ndim)
    out_shape = (
        jax.ShapeDtypeStruct((s, D_MODEL), F32),
        jax.ShapeDtypeStruct((s, D_MODEL), F32),
        jax.ShapeDtypeStruct((SUBLANES, s), I32),
        jax.ShapeDtypeStruct((SUBLANES, s), I32),
        jax.ShapeDtypeStruct((s, LANES), F32),
        jax.ShapeDtypeStruct((N_EXPERTS, LANES), F32),
    )
    out_specs = (row(D_MODEL), row(D_MODEL), col(SUBLANES), col(SUBLANES), row(LANES),
                 pl.BlockSpec((N_EXPERTS, LANES), lambda i: (0, 0)))
    return pl.pallas_call(
        _merge_kernel, grid=(s // tm,),
        in_specs=[row(ATTN_WIDTH), row(MLSTM_WIDTH), row(2 * D_MODEL), row(D_MODEL),
                  full(wa), full(wm), full(wo), full(bg), full(fg), full(wrh), full(wrl), full(rb)],
        out_specs=out_specs, out_shape=out_shape,
        scratch_shapes=[pltpu.VMEM((N_EXPERTS, LANES), F32)],
        compiler_params=_cparams("arbitrary"), name="merge",
    )(ya, ym, gates, x, wa, wm, wo, bg, fg, wrh, wrl, rb)


def _dest_kernel(eid_ref, rank_ref, poff_ref, dest_ref):
    eid = eid_ref[...]
    xidx = lax.broadcasted_iota(I32, (N_EXPERTS, eid.shape[1]), 0)
    poff = poff_ref[:, 0:1]
    dest_ref[...] = jnp.zeros(dest_ref.shape, I32)
    for kk in range(2):
        base = jnp.sum(jnp.where(xidx == eid[kk:kk + 1, :], poff, 0), axis=0, keepdims=True)
        dest_ref[kk:kk + 1, :] = base + rank_ref[kk:kk + 1, :]


def _dest(eid, rank, poff):
    s = eid.shape[1]
    col = pl.BlockSpec((SUBLANES, IDX_TOK), lambda i: (0, i))
    return pl.pallas_call(
        _dest_kernel, grid=(s // IDX_TOK,),
        in_specs=[col, col, pl.BlockSpec(poff.shape, lambda i: (0, 0))],
        out_specs=col, out_shape=jax.ShapeDtypeStruct((SUBLANES, s), I32),
        compiler_params=_cparams("arbitrary"), name="dest",
    )(eid, rank, poff)


def _load_indices(dest_hbm, idx_ref, sem):
    i = pl.program_id(0)
    per = IDX_TOK // TM_TOK

    @pl.when(i % per == 0)
    def _():
        cols = pl.ds(pl.multiple_of((i // per) * IDX_TOK, IDX_TOK), IDX_TOK)
        cp = pltpu.make_async_copy(dest_hbm.at[:, cols], idx_ref, sem)
        cp.start()
        cp.wait()

    return (i % per) * TM_TOK


def _dispatch_kernel(x_ref, dest_hbm, zeros_hbm, xs_hbm, idx_ref, isem, sem):
    del zeros_hbm
    t0 = _load_indices(dest_hbm, idx_ref, isem)

    def issue(r, c):
        for kk in range(2):
            slot = idx_ref[kk, t0 + r]
            pltpu.make_async_copy(x_ref.at[pl.ds(r, 1)], xs_hbm.at[pl.ds(slot, 1)], sem).start()
        return c

    lax.fori_loop(0, TM_TOK, issue, 0, unroll=ISSUE_UNROLL)
    for _ in range(2):
        pltpu.make_async_copy(x_ref, xs_hbm.at[pl.ds(0, TM_TOK)], sem).wait()


def _dispatch(xn, dest, n_pad):
    s = xn.shape[0]
    zeros = jnp.zeros((n_pad, D_MODEL), F32)
    any_spec = pl.BlockSpec(memory_space=pl.ANY)
    return pl.pallas_call(
        _dispatch_kernel, grid=(s // TM_TOK,),
        in_specs=[pl.BlockSpec((TM_TOK, D_MODEL), lambda i: (i, 0)), any_spec, any_spec],
        out_specs=any_spec,
        out_shape=jax.ShapeDtypeStruct((n_pad, D_MODEL), F32),
        scratch_shapes=[pltpu.SMEM((SUBLANES, IDX_TOK), I32),
                        pltpu.SemaphoreType.DMA, pltpu.SemaphoreType.DMA],
        input_output_aliases={2: 0},
        compiler_params=_cparams("arbitrary"), name="dispatch",
    )(xn, dest, zeros)


def _experts_kernel(be_ref, xs_ref, wg_ref, wu_ref, wd_ref, y_ref):
    del be_ref
    xb = xs_ref[...].astype(BF16)
    g = jnp.dot(xb, wg_ref[0].astype(BF16), preferred_element_type=F32)
    u = jnp.dot(xb, wu_ref[0].astype(BF16), preferred_element_type=F32)
    hid = (g * _sigmoid(g) * u).astype(BF16)
    y_ref[...] = jnp.dot(hid, wd_ref[0].astype(BF16), preferred_element_type=F32)


def _experts(blk_e, xs, wg, wu, wd):
    n_pad = xs.shape[0]
    grid_spec = pltpu.PrefetchScalarGridSpec(
        num_scalar_prefetch=1, grid=(n_pad // ROW_BLOCK,),
        in_specs=[pl.BlockSpec((ROW_BLOCK, D_MODEL), lambda b, be: (b, 0)),
                  pl.BlockSpec((1, D_MODEL, EXPERT_FF), lambda b, be: (be[b], 0, 0)),
                  pl.BlockSpec((1, D_MODEL, EXPERT_FF), lambda b, be: (be[b], 0, 0)),
                  pl.BlockSpec((1, EXPERT_FF, D_MODEL), lambda b, be: (be[b], 0, 0))],
        out_specs=pl.BlockSpec((ROW_BLOCK, D_MODEL), lambda b, be: (b, 0)),
    )
    return pl.pallas_call(
        _experts_kernel, grid_spec=grid_spec,
        out_shape=jax.ShapeDtypeStruct((n_pad, D_MODEL), F32),
        compiler_params=_cparams("arbitrary"), name="experts",
    )(blk_e, xs, wg, wu, wd)


def _final_kernel(h1_ref, p_ref, wcol_ref, dest_hbm, yb_hbm, wpg_ref, wpp_ref, pg_ref, fg_ref,
                  out_ref, idx_ref, g_ref, isem, sems):
    i = pl.program_id(0)
    per = IDX_TOK // TM_TOK

    def fetch(tile):
        @pl.when(tile % per == 0)
        def _():
            cols = pl.ds(pl.multiple_of((tile // per) * IDX_TOK, IDX_TOK), IDX_TOK)
            cp = pltpu.make_async_copy(dest_hbm.at[:, cols], idx_ref, isem)
            cp.start()
            cp.wait()

        t0 = (tile % per) * TM_TOK
        slot = tile % 2

        def issue(r, c):
            for kk in range(2):
                row = idx_ref[kk, t0 + r]
                pltpu.make_async_copy(yb_hbm.at[pl.ds(row, 1)], g_ref.at[slot, kk, pl.ds(r, 1)],
                                      sems.at[slot]).start()
            return c

        lax.fori_loop(0, TM_TOK, issue, 0, unroll=ISSUE_UNROLL)

    @pl.when(i == 0)
    def _():
        fetch(i)

    @pl.when(i + 1 < pl.num_programs(0))
    def _():
        fetch(i + 1)

    slot = i % 2
    for kk in range(2):
        pltpu.make_async_copy(yb_hbm.at[pl.ds(0, TM_TOK)], g_ref.at[slot, kk], sems.at[slot]).wait()

    w = wcol_ref[...]
    h2 = h1_ref[...] + (g_ref[slot, 0] * w[:, 0:1] + g_ref[slot, 1] * w[:, 1:2])
    z = _rms(h2, pg_ref[...]).astype(BF16)
    gate = _sigmoid(jnp.dot(z, wpg_ref[...], preferred_element_type=F32))
    pp = jnp.dot(p_ref[...].astype(BF16), wpp_ref[...], preferred_element_type=F32)
    out_ref[...] = _rms(h2 + pp * gate, fg_ref[...])


def _final(h1, p, wcol, dest, yb, wpg, wpp, pg, fg):
    s = h1.shape[0]
    row = lambda w: pl.BlockSpec((TM_TOK, w), lambda i: (i, 0))
    full = lambda a: pl.BlockSpec(a.shape, lambda i: (0,) * a.ndim)
    any_spec = pl.BlockSpec(memory_space=pl.ANY)
    return pl.pallas_call(
        _final_kernel, grid=(s // TM_TOK,),
        in_specs=[row(D_MODEL), row(PLE_DIM), row(LANES), any_spec, any_spec,
                  full(wpg), full(wpp), full(pg), full(fg)],
        out_specs=row(D_MODEL),
        out_shape=jax.ShapeDtypeStruct((s, D_MODEL), F32),
        scratch_shapes=[pltpu.SMEM((SUBLANES, IDX_TOK), I32),
                        pltpu.VMEM((2, 2, TM_TOK, D_MODEL), F32),
                        pltpu.SemaphoreType.DMA, pltpu.SemaphoreType.DMA((2,))],
        compiler_params=_cparams("arbitrary"), name="final",
    )(h1, p, wcol, dest, yb, wpg, wpp, pg, fg)


def _split_bf16(w):
    hi = w.astype(BF16)
    return hi, (w - hi.astype(F32)).astype(BF16)


def _layer(h, p, mix_norm_g, w_in, b_gate, conv_w, conv_b, mlstm_if_b, mlstm_norm_g,
           w_up_attn, w_up_mlstm, w_out, ffn_norm_g, rg_w, rg_b, re_w, re_b,
           w_gate, w_up, w_down, ple_norm_g, w_ple_gate, w_ple_proj, out_norm_g):
    s = h.shape[0]
    aw, mw, d = ATTN_WIDTH, MLSTM_WIDTH, D_MODEL
    o_mq = 3 * aw
    o_if = o_mq + 4 * mw
    o_g = o_if + 2 * MLSTM_HEADS
    w_if = w_in[:, o_if:o_g]
    wrow = jnp.concatenate(
        [w_in[:, aw:2 * aw], w_in[:, o_mq:o_if], w_in[:, o_g:], w_if,
         jnp.zeros((d, LANES - 2 * MLSTM_HEADS), F32)], axis=1).astype(BF16)
    wt = jnp.concatenate(
        [w_in[:, 0:aw].T, w_in[:, 2 * aw:3 * aw].T, w_if.T,
         jnp.zeros((16 - 2 * MLSTM_HEADS, d), F32)], axis=0).astype(BF16)

    k, kstat, mqk, mv, mo, gates, ifc, qT, vT, ifT = _proj(h, mix_norm_g[None, :], wrow, wt)
    ya = _moba2(qT, k, vT, kstat[:, 0, :])

    ifb_row = jnp.concatenate([mlstm_if_b, jnp.zeros((LANES - 2 * MLSTM_HEADS,), F32)])[None, :]
    ifb_col = jnp.broadcast_to(mlstm_if_b[:, None], (2 * MLSTM_HEADS, MLSTM_CHUNK))
    ym = _mlstm(mqk, mv, mo, ifc, ifT, conv_w, conv_b[None, :], ifb_row, ifb_col, mlstm_norm_g[None, :])

    wr = jnp.concatenate([rg_w.T, jnp.zeros((SUBLANES - N_GROUPS, d), F32), re_w.T], axis=0)
    wrh, wrl = _split_bf16(wr)
    rb = jnp.concatenate([rg_b, jnp.zeros((SUBLANES - N_GROUPS,), F32), re_b])
    rb = jnp.broadcast_to(rb[:, None], (_R_ROWS, LANES))
    h1, xn, eid, rank, wcol, cnt = _merge(
        ya, ym, gates, h, w_up_attn.astype(BF16), w_up_mlstm.astype(BF16), w_out.astype(BF16),
        b_gate[None, :], ffn_norm_g[None, :], wrh, wrl, rb)

    n_assign = 2 * s
    n_pad = (n_assign + N_EXPERTS * (ROW_BLOCK - 1) + ROW_BLOCK - 1) // ROW_BLOCK * ROW_BLOCK
    counts = cnt[:, 0].astype(I32)
    pcounts = (counts + ROW_BLOCK - 1) // ROW_BLOCK * ROW_BLOCK
    pends = jnp.cumsum(pcounts)
    poffs = pends - pcounts
    blk_start = jnp.arange(n_pad // ROW_BLOCK, dtype=I32) * ROW_BLOCK
    blk_e = jnp.minimum(jnp.sum((pends[None, :] <= blk_start[:, None]).astype(I32), axis=1), N_EXPERTS - 1)
    dest = _dest(eid, rank, jnp.broadcast_to(poffs[:, None], (N_EXPERTS, LANES)))

    xs = _dispatch(xn, dest, n_pad)
    yb = _experts(blk_e, xs, w_gate, w_up, w_down)
    return _final(h1, p, wcol, dest, yb, w_ple_gate.astype(BF16), w_ple_proj.astype(BF16),
                  ple_norm_g[None, :], out_norm_g[None, :])


def kernel(x, p, mix_norm_g, w_in, b_gate, conv_w, conv_b, mlstm_if_b, mlstm_norm_g, w_up_attn, w_up_mlstm, w_out, ffn_norm_g, router_group_w, router_group_b, router_expert_w, router_expert_b, expert_w_gate, expert_w_up, expert_w_down, ple_norm_g, w_ple_gate, w_ple_proj, final_norm_g):
    assert w_in.shape[0] == 1 and x.shape[0] == 1, "one layer, one sequence"
    assert x.shape[1] % IDX_TOK == 0
    out = _layer(x[0], p[0, 0], mix_norm_g[0], w_in[0], b_gate[0], conv_w[0], conv_b[0], mlstm_if_b[0],
                 mlstm_norm_g[0], w_up_attn[0], w_up_mlstm[0], w_out[0], ffn_norm_g[0],
                 router_group_w[0], router_group_b[0], router_expert_w[0], router_expert_b[0],
                 expert_w_gate[0], expert_w_up[0], expert_w_down[0], ple_norm_g[0], w_ple_gate[0],
                 w_ple_proj[0], final_norm_g)
    return out[None]
```
